```python
import math
import jax, jax.numpy as jnp
from jax import lax
import numpy as np

D_MODEL = 2048
BATCH = 2
SEQ = 4096
DEPTH = 2
DEC_BATCH = 8
DEC_SEQ = 4
PAST_LEN = 16384
PAGE_SIZE = 128

A_HEADS = 8
A_QK = 128
A_DV = 2 * A_QK
A_QBLOCK = 128
M_HEADS = 4
M_DH = D_MODEL // M_HEADS
M_WIDTH = M_HEADS * M_DH
M_CONV = 4
M_CHUNK = 64
D_FF = 5632
FF_CONV = 3
EPS = 1e-6

SPLITS = (A_HEADS * 2 * A_QK,
          A_HEADS * 2 * A_QK,
          A_HEADS * A_DV,
          M_WIDTH,
          M_WIDTH,
          M_WIDTH,
          M_WIDTH,
          2 * M_HEADS,
          2 * D_MODEL)
IN_COLS = sum(SPLITS)

kernel_name = "diffattn_mlstm_gated_hybrid_step"

F32 = jnp.float32


def _rmsnorm(x, g):
    xf = x.astype(F32)
    y = xf * lax.rsqrt(jnp.mean(xf * xf, axis=-1, keepdims=True) + EPS)
    return (y * g.astype(F32)).astype(x.dtype)


def _split_cols(z):
    parts, off = [], 0
    for w in SPLITS:
        parts.append(z[..., off:off + w])
        off += w
    return parts


def _causal_dwconv(x, buf, w):
    width = w.shape[0]
    t = x.shape[1]
    xp = jnp.concatenate([buf.astype(x.dtype), x], axis=1)
    y = sum(xp[:, j:j + t] * w[j].astype(x.dtype) for j in range(width))
    return y, xp[:, xp.shape[1] - (width - 1):]


def _lambda(wl, layer):
    lam_init = 0.8 - 0.6 * math.exp(-0.3 * layer)
    wf = wl.astype(F32)
    lam = jnp.exp(jnp.sum(wf[0] * wf[1])) - jnp.exp(jnp.sum(wf[2] * wf[3])) + lam_init
    return lam, lam_init


def _diff_weights(s, lam):
    p = jax.nn.softmax(s, axis=-1)
    return p[:, 0] - lam * p[:, 1]


def _diff_attn_prompt(q, k, v, lam):
    b, t, h, _, e = q.shape
    nb = t // A_QBLOCK
    scale = e ** -0.5
    qb = q.reshape(b, nb, A_QBLOCK, h, 2, e).transpose(1, 0, 2, 3, 4, 5)
    kf = k.astype(F32)
    vf = v.astype(F32)
    kpos = jnp.arange(t)

    def one_block(args):
        qi, bi = args
        s = jnp.einsum('bqhce,bkhce->bchqk', qi.astype(F32), kf) * scale
        qpos = bi * A_QBLOCK + jnp.arange(A_QBLOCK)
        s = jnp.where(kpos[None, :] <= qpos[:, None], s, -jnp.inf)
        w = _diff_weights(s, lam)
        return jnp.einsum('bhqk,bkhd->bqhd', w, vf)

    o = lax.map(one_block, (qb, jnp.arange(nb)))
    return o.transpose(1, 0, 2, 3, 4).reshape(b, t, h, -1).astype(v.dtype)


def _diff_attn_sample(q, k_new, v_new, k_past, v_past, lam):
    t = q.shape[1]
    p_len = k_past.shape[1]
    qf = q.astype(F32) * (q.shape[-1] ** -0.5)
    s_past = jnp.einsum('bqhce,bkhce->bchqk', qf, k_past.astype(F32))
    s_new = jnp.einsum('bqhce,bkhce->bchqk', qf, k_new.astype(F32))
    s_new = jnp.where(jnp.tril(jnp.ones((t, t), bool)), s_new, -jnp.inf)
    w = _diff_weights(jnp.concatenate([s_past, s_new], axis=-1), lam)
    o = (jnp.einsum('bhqk,bkhd->bqhd', w[..., :p_len], v_past.astype(F32))
         + jnp.einsum('bhqk,bkhd->bqhd', w[..., p_len:], v_new.astype(F32)))
    return o.astype(v_new.dtype)


def _mlstm(q, k, v, i_pre, logf, c0, n0, m0):
    b, t, h, d = q.shape
    L = M_CHUNK if t % M_CHUNK == 0 else t
    nc = t // L
    causal = jnp.tril(jnp.ones((L, L), bool))

    def chunks(a):
        a = a.astype(F32).reshape((b, nc, L) + a.shape[2:])
        return jnp.moveaxis(a, (1, 3), (0, 2))

    def step(carry, xs):
        C, n, m = carry
        qc, kc, vc, ic, fc = xs
        bcum = jnp.cumsum(fc, axis=-1)
        dmat = jnp.where(causal, bcum[..., :, None] - bcum[..., None, :] + ic[..., None, :], -jnp.inf)
        inter = bcum + m[..., None]
        m_t = jnp.maximum(inter, jnp.max(dmat, axis=-1))
        w_intra = jnp.exp(dmat - m_t[..., None])
        w_inter = jnp.exp(inter - m_t)
        s = jnp.einsum('bhtd,bhsd->bhts', qc, kc) * w_intra
        num = w_inter[..., None] * jnp.einsum('bhtd,bhde->bhte', qc, C) + jnp.einsum('bhts,bhse->bhte', s, vc)
        den = w_inter * jnp.einsum('bhtd,bhd->bht', qc, n) + jnp.sum(s, axis=-1)
        hc = num / jnp.maximum(jnp.abs(den), jnp.exp(-m_t))[..., None]
        m_new = m_t[..., -1]
        w_s = jnp.exp(bcum[..., -1:] - bcum + ic - m_new[..., None])
        decay = jnp.exp(bcum[..., -1] + m - m_new)
        C_new = decay[..., None, None] * C + jnp.einsum('bhsd,bhse->bhde', kc * w_s[..., None], vc)
        n_new = decay[..., None] * n + jnp.einsum('bhs,bhsd->bhd', w_s, kc)
        return (C_new, n_new, m_new), hc

    xs = (chunks(q), chunks(k), chunks(v), chunks(i_pre), chunks(logf))
    (C, n, m), hs = lax.scan(step, (c0.astype(F32), n0.astype(F32), m0.astype(F32)), xs)
    hs = jnp.moveaxis(hs, (0, 2), (1, 3)).reshape(b, t, h, d)
    return hs, C, n, m


def _layer(x, layer, lw, qk_buf, ff_buf, c0, n0, m0, past_kv):
    b, t, _ = x.shape
    hin = _rmsnorm(x, lw['g_mix'])
    z = hin @ lw['w_in']
    aq, ak, av, mq, mk, mv, mo, mif, gates = _split_cols(z)
    aq = _rmsnorm(aq.reshape(b, t, A_HEADS, 2, A_QK), lw['g_q'])
    ak = _rmsnorm(ak.reshape(b, t, A_HEADS, 2, A_QK), lw['g_k'])
    av = av.reshape(b, t, A_HEADS, A_DV)
    lam, lam_init = _lambda(lw['w_lambda'], layer)
    if past_kv is None:
        ao = _diff_attn_prompt(aq, ak, av, lam)
    else:
        ao = _diff_attn_sample(aq, ak, av, past_kv[0], past_kv[1], lam)
    ao = (_rmsnorm(ao, lw['g_sub']) * (1.0 - lam_init)).reshape(b, t, A_HEADS * A_DV)
    qk, qk_buf_new = _causal_dwconv(jnp.concatenate([mq, mk], axis=-1), qk_buf, lw['w_qkconv'])
    qk = jax.nn.silu(qk)
    mq_h = qk[..., :M_WIDTH].reshape(b, t, M_HEADS, M_DH)
    mk_h = qk[..., M_WIDTH:].reshape(b, t, M_HEADS, M_DH) * (M_DH ** -0.5)
    mv_h = mv.reshape(b, t, M_HEADS, M_DH)
    gif = mif.astype(F32) + lw['b_if'].reshape(2 * M_HEADS).astype(F32)
    i_pre = gif[..., :M_HEADS]
    logf = jax.nn.log_sigmoid(gif[..., M_HEADS:])
    hm, C, n, m = _mlstm(mq_h, mk_h, mv_h, i_pre, logf, c0, n0, m0)
    hm = _rmsnorm(hm, lw['g_mout'].reshape(M_HEADS, M_DH)) * jax.nn.sigmoid(mo.reshape(b, t, M_HEADS, M_DH).astype(F32))
    hm = hm.reshape(b, t, M_WIDTH).astype(x.dtype)
    g = jax.nn.sigmoid(gates.astype(F32))
    ya = ao @ lw['w_br_attn']
    ym = hm @ lw['w_br_mlstm']
    merged = (g[..., :D_MODEL] * ya + g[..., D_MODEL:] * ym).astype(x.dtype)
    x = x + merged @ lw['w_out']
    h2 = _rmsnorm(x, lw['g_ffn'])
    gu = h2 @ lw['w_ff_in']
    fg, ff_buf_new = _causal_dwconv(gu[..., :D_FF], ff_buf, lw['w_ffconv'])
    x = x + (jax.nn.silu(fg) * gu[..., D_FF:]) @ lw['w_ff_out']
    k_rows = ak.reshape(b, t, A_HEADS, 2 * A_QK)
    return x, (k_rows, av, C, n, m, qk_buf_new, ff_buf_new)


def setup_inputs(seed: int = 0) -> dict:
    key = jax.random.key(seed)
    ks = jax.random.split(key, 32)
    n_pages = PAST_LEN // PAGE_SIZE
    n_used = DEC_BATCH * n_pages
    n_pool = n_used + (n_used + 3) // 4

    def nrm(k, shape, s):
        return jax.random.normal(k, shape, F32) * s

    page_table = jax.random.permutation(ks[4], n_pool)[:n_used].reshape(DEC_BATCH, n_pages).astype(jnp.int32)
    b_if = jnp.stack([nrm(ks[17], (DEPTH, M_HEADS), 0.1),
                      jnp.linspace(3.0, 6.0, M_HEADS, dtype=F32)[None, :] + nrm(ks[18], (DEPTH, M_HEADS), 0.1)], axis=1)
    return {
        'x_prompt': nrm(ks[0], (BATCH, SEQ, D_MODEL), 1.0),
        'x_sample': nrm(ks[1], (DEC_BATCH, DEC_SEQ, D_MODEL), 1.0),
        'cache_k': nrm(ks[2], (DEPTH, n_pool, PAGE_SIZE, A_HEADS, 2 * A_QK), 1.0),
        'cache_v': nrm(ks[3], (DEPTH, n_pool, PAGE_SIZE, A_HEADS, A_DV), 1.0),
        'page_table': page_table,
        'state_C': nrm(ks[5], (DEPTH, DEC_BATCH, M_HEADS, M_DH, M_DH), M_DH ** -0.5),
        'state_n': nrm(ks[6], (DEPTH, DEC_BATCH, M_HEADS, M_DH), 0.5),
        'state_m': nrm(ks[7], (DEPTH, DEC_BATCH, M_HEADS), 1.0),
        'state_qkconv': nrm(ks[8], (DEPTH, DEC_BATCH, M_CONV - 1, 2 * M_WIDTH), 1.0),
        'state_ffconv': nrm(ks[9], (DEPTH, DEC_BATCH, FF_CONV - 1, D_FF), 1.0),
        'g_mix': 1.0 + nrm(ks[10], (DEPTH, D_MODEL), 0.1),
        'w_in': nrm(ks[11], (DEPTH, D_MODEL, IN_COLS), D_MODEL ** -0.5),
        'g_q': 1.0 + nrm(ks[12], (DEPTH, A_QK), 0.1),
        'g_k': 1.0 + nrm(ks[13], (DEPTH, A_QK), 0.1),
        'w_lambda': nrm(ks[14], (DEPTH, 4, A_QK), 0.1),
        'g_sub': 1.0 + nrm(ks[15], (DEPTH, A_DV), 0.1),
        'w_qkconv': nrm(ks[16], (DEPTH, M_CONV, 2 * M_WIDTH), M_CONV ** -0.5),
        'b_if': b_if,
        'g_mout': 1.0 + nrm(ks[19], (DEPTH, M_WIDTH), 0.1),
        'w_br_attn': nrm(ks[20], (DEPTH, A_HEADS * A_DV, D_MODEL), (A_HEADS * A_DV) ** -0.5),
        'w_br_mlstm': nrm(ks[21], (DEPTH, M_WIDTH, D_MODEL), M_WIDTH ** -0.5),
        'w_out': nrm(ks[22], (DEPTH, D_MODEL, D_MODEL), D_MODEL ** -0.5),
        'g_ffn': 1.0 + nrm(ks[23], (DEPTH, D_MODEL), 0.1),
        'w_ff_in': nrm(ks[24], (DEPTH, D_MODEL, 2 * D_FF), D_MODEL ** -0.5),
        'w_ffconv': nrm(ks[25], (DEPTH, FF_CONV, D_FF), FF_CONV ** -0.5),
        'w_ff_out': nrm(ks[26], (DEPTH, D_FF, D_MODEL), D_FF ** -0.5),
    }


def reference(x_prompt, x_sample, cache_k, cache_v, page_table, state_C, state_n, state_m,
              state_qkconv, state_ffconv, g_mix, w_in, g_q, g_k, w_lambda, g_sub, w_qkconv,
              b_if, g_mout, w_br_attn, w_br_mlstm, w_out, g_ffn, w_ff_in, w_ffconv, w_ff_out):
    bp = x_prompt.shape[0]
    bs = x_sample.shape[0]
    past = page_table.shape[1] * cache_k.shape[2]
    zero_qk = jnp.zeros((bp, M_CONV - 1, 2 * M_WIDTH), x_prompt.dtype)
    zero_ff = jnp.zeros((bp, FF_CONV - 1, D_FF), x_prompt.dtype)
    zero_C = jnp.zeros((bp, M_HEADS, M_DH, M_DH), F32)
    zero_n = jnp.zeros((bp, M_HEADS, M_DH), F32)
    zero_m = jnp.zeros((bp, M_HEADS), F32)
    yp, ys = x_prompt, x_sample
    new_p = [[] for _ in range(7)]
    new_s = [[] for _ in range(7)]
    for l in range(DEPTH):
        lw = {'g_mix': g_mix[l], 'w_in': w_in[l], 'g_q': g_q[l], 'g_k': g_k[l],
              'w_lambda': w_lambda[l], 'g_sub': g_sub[l], 'w_qkconv': w_qkconv[l], 'b_if': b_if[l],
              'g_mout': g_mout[l], 'w_br_attn': w_br_attn[l], 'w_br_mlstm': w_br_mlstm[l],
              'w_out': w_out[l], 'g_ffn': g_ffn[l], 'w_ff_in': w_ff_in[l], 'w_ffconv': w_ffconv[l],
              'w_ff_out': w_ff_out[l]}
        yp, sp = _layer(yp, l, lw, zero_qk, zero_ff, zero_C, zero_n, zero_m, None)
        k_past = cache_k[l, page_table].reshape(bs, past, A_HEADS, 2, A_QK)
        v_past = cache_v[l, page_table].reshape(bs, past, A_HEADS, A_DV)
        ys, ss = _layer(ys, l, lw, state_qkconv[l], state_ffconv[l], state_C[l], state_n[l],
                        state_m[l], (k_past, v_past))
        for j in range(7):
            new_p[j].append(sp[j])
            new_s[j].append(ss[j])
    kp, vp, Cp, np_, mp, qkp, ffp = [jnp.stack(a, axis=0) for a in new_p]
    ks_, vs, Cs, ns, ms, qks, ffs = [jnp.stack(a, axis=0) for a in new_s]
    return (yp, ys, kp, vp, Cp, np_, mp, qkp, ffp, ks_, vs, Cs, ns, ms, qks, ffs)
```

```python
import functools
import math
from typing import NamedTuple

import jax
import jax.numpy as jnp
from jax import lax
from jax.experimental import pallas as pl
from jax.experimental.pallas import tpu as pltpu

F32 = jnp.float32
BF16 = jnp.bfloat16
EPS = 1e-6
NEG_INF = float("-inf")

LANES = 128
SUBLANES = 8
SAMPLE_TPAD = 16
VMEM_LIMIT = 52 * 2 ** 20

NT_DIMS = (((1,), (1,)), ((), ()))
NN_DIMS = (((1,), (0,)), ((), ()))
TN_DIMS = (((0,), (0,)), ((), ()))


class Cfg(NamedTuple):
    d: int
    ah: int
    e: int
    dv: int
    mh: int
    dh: int
    conv: int
    dff: int
    ffconv: int

    @property
    def qw(self):
        return self.ah * 2 * self.e

    @property
    def vw(self):
        return self.ah * self.dv

    @property
    def mw(self):
        return self.mh * self.dh


def _tile(n, pref, mult):
    t = min(pref, n)
    t -= t % mult
    while t >= mult:
        if n % t == 0:
            return t
        t -= mult
    return n


def _cparams(*sem):
    return pltpu.CompilerParams(dimension_semantics=sem, vmem_limit_bytes=VMEM_LIMIT)


def _sigmoid(x):
    return 1.0 / (1.0 + jnp.exp(-x))


def _log_sigmoid(x):
    return -(jnp.maximum(-x, 0.0) + jnp.log1p(jnp.exp(-jnp.abs(x))))


def _rms(x, g):
    ms = jnp.mean(x * x, axis=-1, keepdims=True)
    return x * lax.rsqrt(ms + EPS) * g


def _split_bf16(a):
    hi = a.astype(BF16)
    lo = (a - hi.astype(F32)).astype(BF16)
    return hi, lo


def _dot3(a, b, dims):
    ah, al = _split_bf16(a)
    bh, bl = _split_bf16(b)
    f = lambda p, q: lax.dot_general(p, q, dims, preferred_element_type=F32)
    return f(ah, bh) + (f(ah, bl) + f(al, bh))


def _norm_mm_kernel(x_ref, g_ref, w_ref, gn_ref, o_ref, hin_ref, *, n_norm_blocks, tn):
    j = pl.program_id(1)

    @pl.when(j == 0)
    def _():
        hin_ref[...] = _rms(x_ref[...], g_ref[...]).astype(BF16)

    acc = jnp.dot(hin_ref[...], w_ref[...], preferred_element_type=F32)
    if n_norm_blocks == 0:
        o_ref[...] = acc
        return

    @pl.when(j < n_norm_blocks)
    def _():
        for c in range(tn // LANES):
            sl = slice(c * LANES, (c + 1) * LANES)
            o_ref[:, sl] = _rms(acc[:, sl], gn_ref[:, sl])

    @pl.when(j >= n_norm_blocks)
    def _():
        o_ref[...] = acc


def _norm_mm(x, g, w, gn=None, *, name):
    r, k = x.shape
    n = w.shape[1]
    tm = _tile(r, 512, SUBLANES)
    n_norm = 0 if gn is None else gn.shape[1]
    tn = _tile(math.gcd(n, n_norm) if n_norm else n, 1024, LANES)
    nnb = n_norm // tn
    if gn is None:
        gn = jnp.ones((1, tn), F32)
    return pl.pallas_call(
        functools.partial(_norm_mm_kernel, n_norm_blocks=nnb, tn=tn),
        grid=(r // tm, n // tn),
        in_specs=[
            pl.BlockSpec((tm, k), lambda i, j: (i, 0)),
            pl.BlockSpec((1, k), lambda i, j: (0, 0)),
            pl.BlockSpec((k, tn), lambda i, j: (0, j)),
            pl.BlockSpec((1, tn), lambda i, j: (0, jnp.minimum(j, max(nnb - 1, 0)))),
        ],
        out_specs=pl.BlockSpec((tm, tn), lambda i, j: (i, j)),
        out_shape=jax.ShapeDtypeStruct((r, n), F32),
        scratch_shapes=[pltpu.VMEM((tm, k), BF16)],
        compiler_params=_cparams("parallel", "arbitrary"),
        name=name,
    )(x, g, w, gn)


def _gates_kernel(x_ref, g_ref, w_ref, wt_ref, b_ref, bt_ref, o_ref, ot_ref, *, mh):
    hin = _rms(x_ref[...], g_ref[...])
    gif = _dot3(hin, w_ref[...], NN_DIMS) + b_ref[...]
    lane = lax.broadcasted_iota(jnp.int32, gif.shape, 1)
    o_ref[...] = jnp.where(lane < mh, gif, _log_sigmoid(gif))
    gift = _dot3(wt_ref[...], hin, NT_DIMS) + bt_ref[:, :1]
    sub = lax.broadcasted_iota(jnp.int32, gift.shape, 0)
    ot_ref[0] = jnp.where(sub < mh, gift, _log_sigmoid(gift))


def _gates(x, g, wif, b_if, bsz, t, cfg):
    r, k = x.shape
    ng = 2 * cfg.mh
    assert ng <= SUBLANES
    tm = _tile(t, 512, LANES) if t % LANES == 0 else t
    w = jnp.zeros((k, LANES), F32).at[:, :ng].set(wif)
    wt = jnp.zeros((SUBLANES, k), F32).at[:ng].set(wif.T)
    b = jnp.zeros((1, LANES), F32).at[0, :ng].set(b_if)
    bt = jnp.zeros((SUBLANES, LANES), F32).at[:ng].set(jnp.broadcast_to(b_if[:, None], (ng, LANES)))
    nt = t // tm
    return pl.pallas_call(
        functools.partial(_gates_kernel, mh=cfg.mh),
        grid=(bsz, nt),
        in_specs=[
            pl.BlockSpec((tm, k), lambda bi, ti: (bi * nt + ti, 0)),
            pl.BlockSpec((1, k), lambda bi, ti: (0, 0)),
            pl.BlockSpec((k, LANES), lambda bi, ti: (0, 0)),
            pl.BlockSpec((SUBLANES, k), lambda bi, ti: (0, 0)),
            pl.BlockSpec((1, LANES), lambda bi, ti: (0, 0)),
            pl.BlockSpec((SUBLANES, LANES), lambda bi, ti: (0, 0)),
        ],
        out_specs=[
            pl.BlockSpec((tm, LANES), lambda bi, ti: (bi * nt + ti, 0)),
            pl.BlockSpec((1, SUBLANES, tm), lambda bi, ti: (bi, 0, ti)),
        ],
        out_shape=[jax.ShapeDtypeStruct((r, LANES), F32),
                   jax.ShapeDtypeStruct((bsz, SUBLANES, t), F32)],
        compiler_params=_cparams("parallel", "parallel"),
        name="gates",
    )(x, g, w, wt, b, bt)


def _lambda_vec(wl_ref, lam_init):
    wl = wl_ref[...]
    a = jnp.sum(wl[0:1] * wl[1:2], axis=-1, keepdims=True)
    b = jnp.sum(wl[2:3] * wl[3:4], axis=-1, keepdims=True)
    return jnp.exp(a) - jnp.exp(b) + lam_init


def _softmax_update(s, vb, m_prev, l_prev, acc_prev):
    m_new = jnp.maximum(m_prev, jnp.max(s, axis=-1, keepdims=True))
    alpha = jnp.exp(m_prev - m_new)
    p = jnp.exp(s - m_new)
    l_new = alpha * l_prev + jnp.sum(p, axis=-1, keepdims=True)
    acc_new = alpha * acc_prev + jnp.dot(p.astype(BF16), vb, preferred_element_type=F32)
    return m_new, l_new, acc_new


def _diff_finish(o1, l1, o2, l2, lam, gs, lam_init):
    o = o1 * (1.0 / l1) - lam * (o2 * (1.0 / l2))
    return _rms(o, gs) * (1.0 - lam_init)


def _pattn_kernel(q_ref, k_ref, v_ref, wl_ref, gs_ref, o_ref, m_s, l_s, acc_s, *, tq, tk, e, scale, lam_init):
    qi = pl.program_id(2)
    ki = pl.program_id(3)

    @pl.when(ki == 0)
    def _():
        m_s[...] = jnp.full(m_s.shape, NEG_INF, F32)
        l_s[...] = jnp.zeros(l_s.shape, F32)
        acc_s[...] = jnp.zeros(acc_s.shape, F32)

    @pl.when(ki <= qi)
    def _():
        q = q_ref[...]
        k = k_ref[...]
        vb = v_ref[...].astype(BF16)
        row = qi * tq + lax.broadcasted_iota(jnp.int32, (tq, tk), 0)
        col = ki * tk + lax.broadcasted_iota(jnp.int32, (tq, tk), 1)
        keep = col <= row
        for c in range(2):
            qc = (q[:, c * e:(c + 1) * e] * scale).astype(BF16)
            kc = k[:, c * e:(c + 1) * e].astype(BF16)
            s = lax.dot_general(qc, kc, NT_DIMS, preferred_element_type=F32)
            s = jnp.where(keep, s, NEG_INF)
            m_s[c], l_s[c], acc_s[c] = _softmax_update(s, vb, m_s[c], l_s[c], acc_s[c])

    @pl.when(ki == qi)
    def _():
        lam = _lambda_vec(wl_ref, lam_init)
        o_ref[...] = _diff_finish(acc_s[0], l_s[0], acc_s[1], l_s[1], lam, gs_ref[...], lam_init).astype(o_ref.dtype)


def _prompt_attn(z, wl, gs, bsz, t, cfg, lam_init):
    r = z.shape[0]
    e, dv, ah = cfg.e, cfg.dv, cfg.ah
    tq = _tile(t, 512, LANES)
    nq = t // tq
    kb = cfg.qw // (2 * e)
    vb = 2 * cfg.qw // dv
    return pl.pallas_call(
        functools.partial(_pattn_kernel, tq=tq, tk=tq, e=e, scale=e ** -0.5, lam_init=lam_init),
        grid=(bsz, ah, nq, nq),
        in_specs=[
            pl.BlockSpec((tq, 2 * e), lambda b, h, qi, ki: (b * nq + qi, h)),
            pl.BlockSpec((tq, 2 * e), lambda b, h, qi, ki: (b * nq + jnp.minimum(ki, qi), kb + h)),
            pl.BlockSpec((tq, dv), lambda b, h, qi, ki: (b * nq + jnp.minimum(ki, qi), vb + h)),
            pl.BlockSpec((4, e), lambda b, h, qi, ki: (0, 0)),
            pl.BlockSpec((1, dv), lambda b, h, qi, ki: (0, 0)),
        ],
        out_specs=pl.BlockSpec((tq, dv), lambda b, h, qi, ki: (b * nq + qi, h)),
        out_shape=jax.ShapeDtypeStruct((r, cfg.vw), BF16),
        scratch_shapes=[pltpu.VMEM((2, tq, 1), F32), pltpu.VMEM((2, tq, 1), F32), pltpu.VMEM((2, tq, dv), F32)],
        compiler_params=_cparams("parallel", "parallel", "parallel", "arbitrary"),
        name="prompt_attn",
    )(z, z, z, wl, gs)


def _sattn_kernel(pt_ref, q_ref, kn_ref, vn_ref, kp_ref, vp_ref, wl_ref, gs_ref, o_ref, m_s, l_s, acc_s,
                  *, ah, e, dv, tpad, npages, scale, lam_init):
    del pt_ref
    p = pl.program_id(1)

    @pl.when(p == 0)
    def _():
        m_s[...] = jnp.full(m_s.shape, NEG_INF, F32)
        l_s[...] = jnp.zeros(l_s.shape, F32)
        acc_s[...] = jnp.zeros(acc_s.shape, F32)

    lane = lax.broadcasted_iota(jnp.int32, (tpad, 2 * e), 1)

    def stacked_q(h):
        qh = q_ref[:, h * 2 * e:(h + 1) * 2 * e] * scale
        return jnp.concatenate([jnp.where(lane < e, qh, 0.0), jnp.where(lane >= e, qh, 0.0)], axis=0).astype(BF16)

    def update(h, s, vb):
        m_s[h], l_s[h], acc_s[h] = _softmax_update(s, vb, m_s[h], l_s[h], acc_s[h])

    for h in range(ah):
        kh = kp_ref[0, 0, :, h * 2 * e:(h + 1) * 2 * e].astype(BF16)
        vh = vp_ref[0, 0, :, h * dv:(h + 1) * dv].astype(BF16)
        update(h, lax.dot_general(stacked_q(h), kh, NT_DIMS, preferred_element_type=F32), vh)

    @pl.when(p == npages - 1)
    def _():
        lam = _lambda_vec(wl_ref, lam_init)
        qt = lax.broadcasted_iota(jnp.int32, (2 * tpad, tpad), 0) % tpad
        kt = lax.broadcasted_iota(jnp.int32, (2 * tpad, tpad), 1)
        for h in range(ah):
            kh = kn_ref[:, h * 2 * e:(h + 1) * 2 * e].astype(BF16)
            vh = vn_ref[:, h * dv:(h + 1) * dv].astype(BF16)
            s = lax.dot_general(stacked_q(h), kh, NT_DIMS, preferred_element_type=F32)
            update(h, jnp.where(kt <= qt, s, NEG_INF), vh)
            acc, l = acc_s[h], l_s[h]
            o = _diff_finish(acc[:tpad], l[:tpad], acc[tpad:], l[tpad:], lam, gs_ref[...], lam_init)
            o_ref[:, h * dv:(h + 1) * dv] = o.astype(o_ref.dtype)


def _sample_attn(z, cache_k, cache_v, page_table, layer, wl, gs, cfg, lam_init):
    r = z.shape[0]
    e, dv, ah = cfg.e, cfg.dv, cfg.ah
    tpad = SAMPLE_TPAD
    bsz, npages = page_table.shape
    depth, npool, page = cache_k.shape[:3]
    ck = cache_k.reshape(depth, npool, page, cfg.qw)
    cv = cache_v.reshape(depth, npool, page, cfg.vw)
    nqb = cfg.qw // cfg.qw
    del nqb
    grid_spec = pltpu.PrefetchScalarGridSpec(
        num_scalar_prefetch=1,
        grid=(bsz, npages),
        in_specs=[
            pl.BlockSpec((tpad, cfg.qw), lambda b, p, pt: (b, 0)),
            pl.BlockSpec((tpad, cfg.qw), lambda b, p, pt: (b, 1)),
            pl.BlockSpec((tpad, cfg.vw), lambda b, p, pt: (b, 2 * cfg.qw // cfg.vw)),
            pl.BlockSpec((1, 1, page, cfg.qw), lambda b, p, pt: (layer, pt[b * npages + p], 0, 0)),
            pl.BlockSpec((1, 1, page, cfg.vw), lambda b, p, pt: (layer, pt[b * npages + p], 0, 0)),
            pl.BlockSpec((4, e), lambda b, p, pt: (0, 0)),
            pl.BlockSpec((1, dv), lambda b, p, pt: (0, 0)),
        ],
        out_specs=pl.BlockSpec((tpad, cfg.vw), lambda b, p, pt: (b, 0)),
        scratch_shapes=[pltpu.VMEM((ah, 2 * tpad, 1), F32), pltpu.VMEM((ah, 2 * tpad, 1), F32),
                        pltpu.VMEM((ah, 2 * tpad, dv), F32)],
    )
    return pl.pallas_call(
        functools.partial(_sattn_kernel, ah=ah, e=e, dv=dv, tpad=tpad, npages=npages, scale=e ** -0.5,
                          lam_init=lam_init),
        grid_spec=grid_spec,
        out_shape=jax.ShapeDtypeStruct((r, cfg.vw), BF16),
        compiler_params=_cparams("parallel", "arbitrary"),
        name="sample_attn",
    )(page_table.reshape(-1), z, z, z, ck, cv, wl, gs)


def _mlstm_kernel(mq_ref, mk_ref, mv_ref, mo_ref, g_ref, gt_ref, wq_ref, wk_ref, stq_ref, stk_ref,
                  c0_ref, n0_ref, m0_ref, gm_ref, hm_ref, c_ref, n_ref, m_ref, xq_s, xk_s,
                  *, L, dh, mh, conv, t_valid, t_total):
    h = pl.program_id(1)
    c = pl.program_id(2)
    hb = conv - 1
    top = SUBLANES

    @pl.when(c == 0)
    def _():
        xq_s[top - hb:top] = stq_ref[0]
        xk_s[top - hb:top] = stk_ref[0]
        c_ref[0] = c0_ref[0]
        n_ref[0] = n0_ref[0]
        m_ref[0] = m0_ref[0]

    xq_s[top:top + L] = mq_ref[...]
    xk_s[top:top + L] = mk_ref[...]

    def conv_silu(x_s, w_ref):
        y = None
        for j in range(conv):
            term = x_s[top - hb + j:top - hb + j + L] * w_ref[j:j + 1]
            y = term if y is None else y + term
        return y * _sigmoid(y)

    q = conv_silu(xq_s, wq_ref)
    k = conv_silu(xk_s, wk_ref) * dh ** -0.5
    xq_s[top - hb:top] = xq_s[top + L - hb:top + L]
    xk_s[top - hb:top] = xk_s[top + L - hb:top + L]

    g = g_ref[...]
    lane = lax.broadcasted_iota(jnp.int32, g.shape, 1)
    ic_col = jnp.sum(jnp.where(lane == h, g, 0.0), axis=-1, keepdims=True)
    fc_col = jnp.sum(jnp.where(lane == mh + h, g, 0.0), axis=-1, keepdims=True)
    gt = gt_ref[0]
    sub = lax.broadcasted_iota(jnp.int32, gt.shape, 0)
    ic_row = jnp.sum(jnp.where(sub == h, gt, 0.0), axis=0, keepdims=True)
    fc_row = jnp.sum(jnp.where(sub == mh + h, gt, 0.0), axis=0, keepdims=True)
    if t_valid < t_total:
        t_col = c * L + lax.broadcasted_iota(jnp.int32, (L, 1), 0)
        t_row = c * L + lax.broadcasted_iota(jnp.int32, (1, L), 1)
        ic_col = jnp.where(t_col < t_valid, ic_col, NEG_INF)
        fc_col = jnp.where(t_col < t_valid, fc_col, 0.0)
        ic_row = jnp.where(t_row < t_valid, ic_row, NEG_INF)
        fc_row = jnp.where(t_row < t_valid, fc_row, 0.0)

    row = lax.broadcasted_iota(jnp.int32, (L, L), 0)
    col = lax.broadcasted_iota(jnp.int32, (L, L), 1)
    causal = col <= row
    bcum_col = jnp.sum(jnp.where(causal, fc_row, 0.0), axis=-1, keepdims=True)
    bcum_row = jnp.sum(jnp.where(row <= col, fc_col, 0.0), axis=0, keepdims=True)

    m_prev = m_ref[0][:, :1]
    n_prev = n_ref[0]
    c_prev = c_ref[0]

    dmat = jnp.where(causal, bcum_col - bcum_row + ic_row, NEG_INF)
    inter = bcum_col + m_prev
    m_t = jnp.maximum(inter, jnp.max(dmat, axis=-1, keepdims=True))
    w_intra = jnp.exp(dmat - m_t)
    w_inter = jnp.exp(inter - m_t)

    qb = q.astype(BF16)
    kb = k.astype(BF16)
    vb = mv_ref[...].astype(BF16)
    s = lax.dot_general(qb, kb, NT_DIMS, preferred_element_type=F32) * w_intra
    num = (w_inter * jnp.dot(qb, c_prev.astype(BF16), preferred_element_type=F32)
           + jnp.dot(s.astype(BF16), vb, preferred_element_type=F32))
    den = w_inter * jnp.sum(q * n_prev, axis=-1, keepdims=True) + jnp.sum(s, axis=-1, keepdims=True)
    hc = num * (1.0 / jnp.maximum(jnp.abs(den), jnp.exp(-m_t)))

    m_new = m_t[L - 1:L]
    b_last = bcum_col[L - 1:L]
    w_s = jnp.exp(b_last - bcum_col + ic_col - m_new)
    decay = jnp.exp(b_last + m_prev - m_new)
    kw = k * w_s
    c_ref[0] = decay * c_prev + lax.dot_general(kw.astype(BF16), vb, TN_DIMS, preferred_element_type=F32)
    n_ref[0] = decay * n_prev + jnp.sum(kw, axis=0, keepdims=True)
    m_ref[0] = jnp.broadcast_to(m_new, m_ref.shape[1:])

    hm_ref[...] = (_rms(hc, gm_ref[...]) * _sigmoid(mo_ref[...])).astype(hm_ref.dtype)


def _mlstm(z, g, gt, wqk, st_qk, c0, n0, m0, g_mout, bsz, t, t_valid, cfg):
    r = z.shape[0]
    dh, mh, conv = cfg.dh, cfg.mh, cfg.conv
    L = _tile(t, 256, LANES) if t % LANES == 0 else t
    nc = t // L
    o_mq = (2 * cfg.qw + cfg.vw) // dh
    bh = bsz * mh
    hb = conv - 1
    m0b = jnp.broadcast_to(m0.reshape(bh, 1, 1), (bh, 1, LANES))
    outs = pl.pallas_call(
        functools.partial(_mlstm_kernel, L=L, dh=dh, mh=mh, conv=conv, t_valid=t_valid, t_total=t),
        grid=(bsz, mh, nc),
        in_specs=[
            pl.BlockSpec((L, dh), lambda b, h, c: (b * nc + c, o_mq + h)),
            pl.BlockSpec((L, dh), lambda b, h, c: (b * nc + c, o_mq + mh + h)),
            pl.BlockSpec((L, dh), lambda b, h, c: (b * nc + c, o_mq + 2 * mh + h)),
            pl.BlockSpec((L, dh), lambda b, h, c: (b * nc + c, o_mq + 3 * mh + h)),
            pl.BlockSpec((L, LANES), lambda b, h, c: (b * nc + c, 0)),
            pl.BlockSpec((1, SUBLANES, L), lambda b, h, c: (b, 0, c)),
            pl.BlockSpec((conv, dh), lambda b, h, c: (0, h)),
            pl.BlockSpec((conv, dh), lambda b, h, c: (0, mh + h)),
            pl.BlockSpec((1, hb, dh), lambda b, h, c: (b, 0, h)),
            pl.BlockSpec((1, hb, dh), lambda b, h, c: (b, 0, mh + h)),
            pl.BlockSpec((1, dh, dh), lambda b, h, c: (b * mh + h, 0, 0)),
            pl.BlockSpec((1, 1, dh), lambda b, h, c: (b * mh + h, 0, 0)),
            pl.BlockSpec((1, 1, LANES), lambda b, h, c: (b * mh + h, 0, 0)),
            pl.BlockSpec((1, dh), lambda b, h, c: (0, h)),
        ],
        out_specs=[
            pl.BlockSpec((L, dh), lambda b, h, c: (b * nc + c, h)),
            pl.BlockSpec((1, dh, dh), lambda b, h, c: (b * mh + h, 0, 0)),
            pl.BlockSpec((1, 1, dh), lambda b, h, c: (b * mh + h, 0, 0)),
            pl.BlockSpec((1, 1, LANES), lambda b, h, c: (b * mh + h, 0, 0)),
        ],
        out_shape=[
            jax.ShapeDtypeStruct((r, cfg.mw), BF16),
            jax.ShapeDtypeStruct((bh, dh, dh), F32),
            jax.ShapeDtypeStruct((bh, 1, dh), F32),
            jax.ShapeDtypeStruct((bh, 1, LANES), F32),
        ],
        scratch_shapes=[pltpu.VMEM((L + SUBLANES, dh), F32), pltpu.VMEM((L + SUBLANES, dh), F32)],
        compiler_params=_cparams("parallel", "parallel", "arbitrary"),
        name="mlstm",
    )(z, z, z, z, g, gt, wqk, wqk, st_qk, st_qk, c0.reshape(bh, dh, dh), n0.reshape(bh, 1, dh), m0b,
      g_mout.reshape(1, cfg.mw))
    hm, c_new, n_new, m_new = outs
    return (hm, c_new.reshape(bsz, mh, dh, dh), n_new.reshape(bsz, mh, dh), m_new[:, 0, 0].reshape(bsz, mh))


def _gated_dual_kernel(ao_ref, hm_ref, ga_ref, gm_ref, wa_ref, wm_ref, o_ref):
    ya = jnp.dot(ao_ref[...].astype(BF16), wa_ref[...], preferred_element_type=F32)
    ym = jnp.dot(hm_ref[...].astype(BF16), wm_ref[...], preferred_element_type=F32)
    o_ref[...] = (_sigmoid(ga_ref[...]) * ya + _sigmoid(gm_ref[...]) * ym).astype(o_ref.dtype)


def _gated_dual(ao, hm, z, wa, wm, gate_off, cfg):
    r = ao.shape[0]
    n = wa.shape[1]
    tm = _tile(r, 512, 2 * SUBLANES)
    tn = _tile(n, 512, LANES)
    ga_b = gate_off // tn
    gm_b = (gate_off + n) // tn
    return pl.pallas_call(
        _gated_dual_kernel,
        grid=(r // tm, n // tn),
        in_specs=[
            pl.BlockSpec((tm, ao.shape[1]), lambda i, j: (i, 0)),
            pl.BlockSpec((tm, hm.shape[1]), lambda i, j: (i, 0)),
            pl.BlockSpec((tm, tn), lambda i, j: (i, ga_b + j)),
            pl.BlockSpec((tm, tn), lambda i, j: (i, gm_b + j)),
            pl.BlockSpec((wa.shape[0], tn), lambda i, j: (0, j)),
            pl.BlockSpec((wm.shape[0], tn), lambda i, j: (0, j)),
        ],
        out_specs=pl.BlockSpec((tm, tn), lambda i, j: (i, j)),
        out_shape=jax.ShapeDtypeStruct((r, n), BF16),
        compiler_params=_cparams("parallel", "arbitrary"),
        name="gated_dual",
    )(ao, hm, z, z, wa, wm)


def _mm_res_kernel(a_ref, w_ref, x_ref, o_ref):
    o_ref[...] = x_ref[...] + jnp.dot(a_ref[...].astype(BF16), w_ref[...], preferred_element_type=F32)


def _mm_res(a, w, x, *, name):
    r, k = a.shape
    n = w.shape[1]
    tm = _tile(r, 512, 2 * SUBLANES)
    tn = _tile(n, 512, LANES)
    return pl.pallas_call(
        _mm_res_kernel,
        grid=(r // tm, n // tn),
        in_specs=[
            pl.BlockSpec((tm, k), lambda i, j: (i, 0)),
            pl.BlockSpec((k, tn), lambda i, j: (0, j)),
            pl.BlockSpec((tm, tn), lambda i, j: (i, j)),
        ],
        out_specs=pl.BlockSpec((tm, tn), lambda i, j: (i, j)),
        out_shape=jax.ShapeDtypeStruct((r, n), F32),
        compiler_params=_cparams("parallel", "arbitrary"),
        name=name,
    )(a, w, x)


def _convact_kernel(gate_ref, up_ref, halo_ref, st_ref, w_ref, o_ref, xp_s, *, nseq, tp, conv, blocks_per_seq):
    i = pl.program_id(0)
    hb = conv - 1
    top = SUBLANES
    for s in range(nseq):
        if blocks_per_seq > 1:
            first = (i % blocks_per_seq) == 0
            prev = jnp.where(first, st_ref[s], halo_ref[SUBLANES - hb:SUBLANES])
        else:
            prev = st_ref[s]
        xp_s[top - hb:top] = prev
        xp_s[top:top + tp] = gate_ref[s * tp:(s + 1) * tp]
        y = None
        for j in range(conv):
            term = xp_s[top - hb + j:top - hb + j + tp] * w_ref[j:j + 1]
            y = term if y is None else y + term
        o_ref[s * tp:(s + 1) * tp] = (y * _sigmoid(y) * up_ref[s * tp:(s + 1) * tp]).astype(o_ref.dtype)


def _convact(gu, st, w, bsz, t, cfg):
    r = gu.shape[0]
    dff, conv = cfg.dff, cfg.ffconv
    tc = _tile(dff, 512, LANES)
    ncb = dff // tc
    if t >= 256:
        tm = _tile(t, 512, 2 * SUBLANES)
        nseq, tp, bps = 1, tm, t // tm
    else:
        tm, nseq, tp, bps = r, bsz, t, 1
    hrows = tm // SUBLANES
    return pl.pallas_call(
        functools.partial(_convact_kernel, nseq=nseq, tp=tp, conv=conv, blocks_per_seq=bps),
        grid=(r // tm, ncb),
        in_specs=[
            pl.BlockSpec((tm, tc), lambda i, j: (i, j)),
            pl.BlockSpec((tm, tc), lambda i, j: (i, ncb + j)),
            pl.BlockSpec((SUBLANES, tc), lambda i, j: (jnp.maximum(i * hrows - 1, 0), j)),
            pl.BlockSpec((nseq, conv - 1, tc), lambda i, j: (i // bps if nseq == 1 else 0, 0, j)),
            pl.BlockSpec((conv, tc), lambda i, j: (0, j)),
        ],
        out_specs=pl.BlockSpec((tm, tc), lambda i, j: (i, j)),
        out_shape=jax.ShapeDtypeStruct((r, dff), BF16),
        scratch_shapes=[pltpu.VMEM((tp + SUBLANES, tc), F32)],
        compiler_params=_cparams("parallel", "parallel"),
        name="convact",
    )(gu, gu, gu, st, w)


def _layer(x, bsz, t, t_valid, layer, lw, st_qk, st_ff, c0, n0, m0, attn, cfg):
    d = cfg.d
    qw, vw, mw = cfg.qw, cfg.vw, cfg.mw
    lam_init = 0.8 - 0.6 * math.exp(-0.3 * layer)
    gate_off = 2 * qw + vw + 4 * mw

    z = _norm_mm(x, lw['g_mix'], lw['w_main'], lw['g_qk'], name="in_proj")
    g, gt = _gates(x, lw['g_mix'], lw['w_if'], lw['b_if'], bsz, t, cfg)
    ao = attn(z, lw['w_lambda'], lw['g_sub'], lam_init)
    hm, c_new, n_new, m_new = _mlstm(z, g, gt, lw['w_qkconv'], st_qk, c0, n0, m0, lw['g_mout'], bsz, t, t_valid, cfg)
    merged = _gated_dual(ao, hm, z, lw['w_br_attn'], lw['w_br_mlstm'], gate_off, cfg)
    x1 = _mm_res(merged, lw['w_out'], x, name="out_proj")
    gu = _norm_mm(x1, lw['g_ffn'], lw['w_ff_in'], name="ffn_up")
    act = _convact(gu, st_ff, lw['w_ffconv'], bsz, t, cfg)
    x2 = _mm_res(act, lw['w_ff_out'], x1, name="ffn_down")

    z3 = z.reshape(bsz, t, -1)
    k_rows = z3[:, :t_valid, qw:2 * qw].reshape(bsz, t_valid, cfg.ah, 2 * cfg.e)
    v_rows = z3[:, :t_valid, 2 * qw:2 * qw + vw].reshape(bsz, t_valid, cfg.ah, cfg.dv)
    o_mq = 2 * qw + vw
    qk_new = z3[:, t_valid - (cfg.conv - 1):t_valid, o_mq:o_mq + 2 * mw]
    ff_new = gu.reshape(bsz, t, -1)[:, t_valid - (cfg.ffconv - 1):t_valid, :cfg.dff]
    del d
    return x2, (k_rows, v_rows, c_new, n_new, m_new, qk_new, ff_new)


def _layer_weights(l, cfg, g_mix, w_in, g_q, g_k, w_lambda, g_sub, w_qkconv, b_if, g_mout, w_br_attn,
                   w_br_mlstm, w_out, g_ffn, w_ff_in, w_ffconv, w_ff_out):
    qw, vw, mw, d = cfg.qw, cfg.vw, cfg.mw, cfg.d
    o_if = 2 * qw + vw + 4 * mw
    ng = 2 * cfg.mh
    w = w_in[l]
    reps = qw // cfg.e
    return {
        'g_mix': g_mix[l].reshape(1, d),
        'w_main': jnp.concatenate([w[:, :o_if], w[:, o_if + ng:]], axis=1).astype(BF16),
        'w_if': w[:, o_if:o_if + ng],
        'b_if': b_if[l].reshape(ng),
        'g_qk': jnp.concatenate([jnp.tile(g_q[l], reps), jnp.tile(g_k[l], reps)]).reshape(1, 2 * qw),
        'w_lambda': w_lambda[l],
        'g_sub': g_sub[l].reshape(1, cfg.dv),
        'w_qkconv': w_qkconv[l],
        'g_mout': g_mout[l],
        'w_br_attn': w_br_attn[l].astype(BF16),
        'w_br_mlstm': w_br_mlstm[l].astype(BF16),
        'w_out': w_out[l].astype(BF16),
        'g_ffn': g_ffn[l].reshape(1, d),
        'w_ff_in': w_ff_in[l].astype(BF16),
        'w_ffconv': w_ffconv[l],
        'w_ff_out': w_ff_out[l].astype(BF16),
    }


def kernel(x_prompt, x_sample, cache_k, cache_v, page_table, state_C, state_n, state_m, state_qkconv,
           state_ffconv, g_mix, w_in, g_q, g_k, w_lambda, g_sub, w_qkconv, b_if, g_mout, w_br_attn,
           w_br_mlstm, w_out, g_ffn, w_ff_in, w_ffconv, w_ff_out):
    bp, tp_, d = x_prompt.shape
    bs, ts, _ = x_sample.shape
    depth = w_in.shape[0]
    cfg = Cfg(d=d, ah=cache_k.shape[3], e=cache_k.shape[4] // 2, dv=cache_v.shape[4], mh=state_C.shape[2],
              dh=state_C.shape[3], conv=w_qkconv.shape[1], dff=w_ff_out.shape[1], ffconv=w_ffconv.shape[1])
    assert ts <= SAMPLE_TPAD and ts >= max(cfg.conv, cfg.ffconv) - 1

    yp = x_prompt.reshape(bp * tp_, d)
    ys = jnp.zeros((bs, SAMPLE_TPAD, d), F32).at[:, :ts].set(x_sample).reshape(bs * SAMPLE_TPAD, d)
    zero_qk = jnp.zeros((bp, cfg.conv - 1, 2 * cfg.mw), F32)
    zero_ff = jnp.zeros((bp, cfg.ffconv - 1, cfg.dff), F32)
    zero_c = jnp.zeros((bp, cfg.mh, cfg.dh, cfg.dh), F32)
    zero_n = jnp.zeros((bp, cfg.mh, cfg.dh), F32)
    zero_m = jnp.zeros((bp, cfg.mh), F32)

    new_p = [[] for _ in range(7)]
    new_s = [[] for _ in range(7)]
    for l in range(depth):
        lw = _layer_weights(l, cfg, g_mix, w_in, g_q, g_k, w_lambda, g_sub, w_qkconv, b_if, g_mout,
                            w_br_attn, w_br_mlstm, w_out, g_ffn, w_ff_in, w_ffconv, w_ff_out)
        p_attn = functools.partial(_prompt_attn, bsz=bp, t=tp_, cfg=cfg)
        yp, sp = _layer(yp, bp, tp_, tp_, l, lw, zero_qk, zero_ff, zero_c, zero_n, zero_m,
                        lambda z, wl, gs, li: p_attn(z, wl, gs, lam_init=li), cfg)
        s_attn = functools.partial(_sample_attn, cache_k=cache_k, cache_v=cache_v, page_table=page_table,
                                   layer=l, cfg=cfg)
        ys, ss = _layer(ys, bs, SAMPLE_TPAD, ts, l, lw, state_qkconv[l], state_ffconv[l], state_C[l],
                        state_n[l], state_m[l], lambda z, wl, gs, li: s_attn(z, wl=wl, gs=gs, lam_init=li), cfg)
        for j in range(7):
            new_p[j].append(sp[j])
            new_s[j].append(ss[j])
    outs_p = [jnp.stack(a, axis=0) for a in new_p]
    outs_s = [jnp.stack(a, axis=0) for a in new_s]
    y_prompt = yp.reshape(bp, tp_, d)
    y_sample = ys.reshape(bs, SAMPLE_TPAD, d)[:, :ts]
    return (y_prompt, y_sample, *outs_p, *outs_s)
```

```python
import functools
import math
from typing import NamedTuple

import numpy as np
import jax
import jax.numpy as jnp
from jax import lax
from jax.experimental import pallas as pl
from jax.experimental.pallas import tpu as pltpu

F32 = jnp.float32
BF16 = jnp.bfloat16
EPS = 1e-6
NEG_INF = float("-inf")
LOG2E = math.log2(math.e)

LANES = 128
SUBLANES = 8
SAMPLE_TPAD = 16
VMEM_LIMIT = 52 * 2 ** 20

NT_DIMS = (((1,), (1,)), ((), ()))
NN_DIMS = (((1,), (0,)), ((), ()))
TN_DIMS = (((0,), (0,)), ((), ()))


class Cfg(NamedTuple):
    d: int
    ah: int
    e: int
    dv: int
    mh: int
    dh: int
    conv: int
    dff: int
    ffconv: int

    @property
    def qw(self):
        return self.ah * 2 * self.e

    @property
    def vw(self):
        return self.ah * self.dv

    @property
    def mw(self):
        return self.mh * self.dh


def _tile(n, pref, mult):
    t = min(pref, n)
    t -= t % mult
    while t >= mult:
        if n % t == 0:
            return t
        t -= mult
    return n


def _cparams(*sem):
    return pltpu.CompilerParams(dimension_semantics=sem, vmem_limit_bytes=VMEM_LIMIT)


def _sigmoid(x):
    return 1.0 / (1.0 + jnp.exp(-x))


def _log_sigmoid(x):
    return -(jnp.maximum(-x, 0.0) + jnp.log1p(jnp.exp(-jnp.abs(x))))


def _rms(x, g):
    ms = jnp.mean(x * x, axis=-1, keepdims=True)
    return x * lax.rsqrt(ms + EPS) * g


def _split_bf16(a):
    hi = a.astype(BF16)
    lo = (a - hi.astype(F32)).astype(BF16)
    return hi, lo


def _dot3(a, b, dims):
    ah, al = _split_bf16(a)
    bh, bl = _split_bf16(b)
    f = lambda p, q: lax.dot_general(p, q, dims, preferred_element_type=F32)
    return f(ah, bh) + (f(ah, bl) + f(al, bh))


def _causal_conv(xp_s, w_ref, top, hb, rows):
    y = None
    for j in range(hb + 1):
        term = xp_s[top - hb + j:top - hb + j + rows] * w_ref[j:j + 1]
        y = term if y is None else y + term
    return y


def _in_proj_kernel(*refs, nqb, nvb, na, q_scale, tn):
    x_ref, g_ref, wa_ref, wg_ref, gn_ref = refs[:5]
    z_ref, side_ref, k_ref, v_ref, hin_ref = refs[-5:]
    j = pl.program_id(1)
    jv0 = 2 * nqb
    j0 = jv0 + nvb

    @pl.when(j == 0)
    def _():
        hin_ref[...] = _rms(x_ref[...], g_ref[...]).astype(BF16)

    def proj(w_ref):
        return jnp.dot(hin_ref[...], w_ref[...], preferred_element_type=F32)

    def qk_norm(side_scale, f32_ref):
        acc = proj(wa_ref)
        for c in range(tn // LANES):
            sl = slice(c * LANES, (c + 1) * LANES)
            y = _rms(acc[:, sl], gn_ref[:, sl])
            if f32_ref is not None:
                f32_ref[:, sl] = y
            side_ref[:, sl] = (y if side_scale is None else y * side_scale).astype(BF16)

    @pl.when(j < nqb)
    def _():
        qk_norm(q_scale, None)

    @pl.when((j >= nqb) & (j < jv0))
    def _():
        qk_norm(None, k_ref)

    @pl.when((j >= jv0) & (j < j0))
    def _():
        acc = proj(wa_ref)
        v_ref[...] = acc
        side_ref[...] = acc.astype(BF16)

    @pl.when((j >= j0) & (j < na))
    def _():
        z_ref[...] = proj(wa_ref)

    @pl.when(j >= na)
    def _():
        z_ref[...] = proj(wg_ref)


def _in_proj(x, g, wa, wg, gn, kv_all, layer, depth, cfg):
    r, k = x.shape
    qw, vw = cfg.qw, cfg.vw
    n_side = 2 * qw + vw
    n_z = wa.shape[1] - n_side + wg.shape[1]
    tm = _tile(r, 1024, SUBLANES)
    tn = _tile(functools.reduce(math.gcd, (qw, vw, wa.shape[1], wg.shape[1])), 512, LANES)
    nqb, nvb, na, nsb = qw // tn, vw // tn, wa.shape[1] // tn, n_side // tn
    nrb = r // tm
    in_specs = [
        pl.BlockSpec((tm, k), lambda i, j: (i, 0)),
        pl.BlockSpec((1, k), lambda i, j: (0, 0)),
        pl.BlockSpec((k, tn), lambda i, j: (0, jnp.minimum(j, na - 1))),
        pl.BlockSpec((k, tn), lambda i, j: (0, jnp.maximum(j - na, 0))),
        pl.BlockSpec((1, tn), lambda i, j: (0, jnp.minimum(j, 2 * nqb - 1))),
    ]
    args = [x, g, wa, wg, gn]
    aliases = {}
    if kv_all is not None:
        in_specs += [pl.BlockSpec(memory_space=pl.ANY), pl.BlockSpec(memory_space=pl.ANY)]
        args += list(kv_all)
        aliases = {5: 2, 6: 3}
    return pl.pallas_call(
        functools.partial(_in_proj_kernel, nqb=nqb, nvb=nvb, na=na, q_scale=cfg.e ** -0.5 * LOG2E, tn=tn),
        grid=(nrb, na + wg.shape[1] // tn),
        in_specs=in_specs,
        out_specs=[
            pl.BlockSpec((tm, tn), lambda i, j: (i, jnp.maximum(j - nsb, 0))),
            pl.BlockSpec((tm, tn), lambda i, j: (i, jnp.minimum(j, nsb - 1))),
            pl.BlockSpec((tm, tn), lambda i, j: (layer * nrb + i, jnp.clip(j - nqb, 0, nqb - 1))),
            pl.BlockSpec((tm, tn), lambda i, j: (layer * nrb + i, jnp.clip(j - 2 * nqb, 0, nvb - 1))),
        ],
        out_shape=[jax.ShapeDtypeStruct((r, n_z), F32), jax.ShapeDtypeStruct((r, n_side), BF16),
                   jax.ShapeDtypeStruct((depth * r, qw), F32), jax.ShapeDtypeStruct((depth * r, vw), F32)],
        scratch_shapes=[pltpu.VMEM((tm, k), BF16)],
        input_output_aliases=aliases,
        compiler_params=_cparams("parallel", "arbitrary"),
        name="in_proj",
    )(*args)


def _gates_kernel(x_ref, g_ref, w_ref, wt_ref, b_ref, bt_ref, o_ref, ot_ref, *, mh):
    hin = _rms(x_ref[...], g_ref[...])
    gif = _dot3(hin, w_ref[...], NN_DIMS) + b_ref[...]
    lane = lax.broadcasted_iota(jnp.int32, gif.shape, 1)
    o_ref[...] = jnp.where(lane < mh, gif, _log_sigmoid(gif))
    gift = _dot3(wt_ref[...], hin, NT_DIMS) + bt_ref[:, :1]
    sub = lax.broadcasted_iota(jnp.int32, gift.shape, 0)
    ot_ref[0] = jnp.where(sub < mh, gift, _log_sigmoid(gift))


def _gates(x, g, wif, b_if, bsz, t, cfg):
    r, k = x.shape
    ng = 2 * cfg.mh
    assert ng <= SUBLANES
    tm = _tile(t, 512, LANES) if t % LANES == 0 else t
    w = jnp.zeros((k, LANES), F32).at[:, :ng].set(wif)
    wt = jnp.zeros((SUBLANES, k), F32).at[:ng].set(wif.T)
    b = jnp.zeros((1, LANES), F32).at[0, :ng].set(b_if)
    bt = jnp.zeros((SUBLANES, LANES), F32).at[:ng].set(jnp.broadcast_to(b_if[:, None], (ng, LANES)))
    nt = t // tm
    return pl.pallas_call(
        functools.partial(_gates_kernel, mh=cfg.mh),
        grid=(bsz, nt),
        in_specs=[
            pl.BlockSpec((tm, k), lambda bi, ti: (bi * nt + ti, 0)),
            pl.BlockSpec((1, k), lambda bi, ti: (0, 0)),
            pl.BlockSpec((k, LANES), lambda bi, ti: (0, 0)),
            pl.BlockSpec((SUBLANES, k), lambda bi, ti: (0, 0)),
            pl.BlockSpec((1, LANES), lambda bi, ti: (0, 0)),
            pl.BlockSpec((SUBLANES, LANES), lambda bi, ti: (0, 0)),
        ],
        out_specs=[
            pl.BlockSpec((tm, LANES), lambda bi, ti: (bi * nt + ti, 0)),
            pl.BlockSpec((1, SUBLANES, tm), lambda bi, ti: (bi, 0, ti)),
        ],
        out_shape=[jax.ShapeDtypeStruct((r, LANES), F32),
                   jax.ShapeDtypeStruct((bsz, SUBLANES, t), F32)],
        compiler_params=_cparams("parallel", "parallel"),
        name="gates",
    )(x, g, w, wt, b, bt)


def _lambda_vec(wl_ref, lam_init):
    wl = wl_ref[...]
    a = jnp.sum(wl[0:1] * wl[1:2], axis=-1, keepdims=True)
    b = jnp.sum(wl[2:3] * wl[3:4], axis=-1, keepdims=True)
    return jnp.exp(a) - jnp.exp(b) + lam_init


def _pattn_kernel(qi_ref, ki_ref, q_ref, k_ref, vt_ref, wl_ref, gs_ref, o_ref, m_s, l_s, acc_s,
                  *, tq, e, lam_init):
    step = pl.program_id(2)
    qi = qi_ref[step]
    ki = ki_ref[step]

    @pl.when(ki == 0)
    def _():
        m_s[...] = jnp.full(m_s.shape, NEG_INF, F32)
        l_s[...] = jnp.zeros(l_s.shape, F32)
        acc_s[...] = jnp.zeros(acc_s.shape, F32)

    def body(masked):
        q = q_ref[...]
        k = k_ref[...]
        vt = vt_ref[...]
        if masked:
            key = lax.broadcasted_iota(jnp.int32, (tq, tq), 0)
            qry = lax.broadcasted_iota(jnp.int32, (tq, tq), 1)
            keep = key <= qry
        for c in range(2):
            st = lax.dot_general(k[:, c * e:(c + 1) * e], q[:, c * e:(c + 1) * e], NT_DIMS,
                                 preferred_element_type=F32)
            if masked:
                st = jnp.where(keep, st, NEG_INF)
            m_prev = m_s[c]
            m_new = jnp.maximum(m_prev, jnp.max(st, axis=0, keepdims=True))
            alpha = jnp.exp2(m_prev - m_new)
            pt = jnp.exp2(st - m_new)
            l_s[c] = alpha * l_s[c] + jnp.sum(pt, axis=0, keepdims=True)
            m_s[c] = m_new
            acc_s[c] = alpha * acc_s[c] + jnp.dot(vt, pt.astype(BF16), preferred_element_type=F32)

    @pl.when(ki < qi)
    def _():
        body(False)

    @pl.when(ki == qi)
    def _():
        body(True)
        lam = _lambda_vec(wl_ref, lam_init)
        ot = acc_s[0] * (1.0 / l_s[0]) - lam * (acc_s[1] * (1.0 / l_s[1]))
        ms = jnp.mean(ot * ot, axis=0, keepdims=True)
        ot = ot * lax.rsqrt(ms + EPS) * gs_ref[...] * (1.0 - lam_init)
        o_ref[...] = ot.T.astype(o_ref.dtype)


def _prompt_attn(side, wl, gs, bsz, t, cfg, lam_init):
    r = side.shape[0]
    e, dv, ah = cfg.e, cfg.dv, cfg.ah
    tq = _tile(t, 512, LANES)
    nq = t // tq
    kb = cfg.qw // (2 * e)
    pairs = [(qi, ki) for qi in range(nq) for ki in range(qi + 1)]
    qi_tab = jnp.asarray(np.array([p[0] for p in pairs], np.int32))
    ki_tab = jnp.asarray(np.array([p[1] for p in pairs], np.int32))
    vt = side[:, 2 * cfg.qw:].reshape(bsz, t, ah, dv).transpose(0, 2, 3, 1).reshape(bsz * ah * dv, t)
    grid_spec = pltpu.PrefetchScalarGridSpec(
        num_scalar_prefetch=2,
        grid=(bsz, ah, len(pairs)),
        in_specs=[
            pl.BlockSpec((tq, 2 * e), lambda b, h, s, qt, kt: (b * nq + qt[s], h)),
            pl.BlockSpec((tq, 2 * e), lambda b, h, s, qt, kt: (b * nq + kt[s], kb + h)),
            pl.BlockSpec((dv, tq), lambda b, h, s, qt, kt: (b * ah + h, kt[s])),
            pl.BlockSpec((4, e), lambda b, h, s, qt, kt: (0, 0)),
            pl.BlockSpec((dv, 1), lambda b, h, s, qt, kt: (0, 0)),
        ],
        out_specs=pl.BlockSpec((tq, dv), lambda b, h, s, qt, kt: (b * nq + qt[s], h)),
        scratch_shapes=[pltpu.VMEM((2, 1, tq), F32), pltpu.VMEM((2, 1, tq), F32), pltpu.VMEM((2, dv, tq), F32)],
    )
    return pl.pallas_call(
        functools.partial(_pattn_kernel, tq=tq, e=e, lam_init=lam_init),
        grid_spec=grid_spec,
        out_shape=jax.ShapeDtypeStruct((r, cfg.vw), BF16),
        compiler_params=_cparams("parallel", "parallel", "arbitrary"),
        name="prompt_attn",
    )(qi_tab, ki_tab, side, side, vt, wl, gs.reshape(dv, 1))


def _sattn_kernel(pt_ref, q_ref, kn_ref, vn_ref, kp_ref, vp_ref, wl_ref, gs_ref, o_ref, qs_s, bias_s, m_s, l_s,
                  acc_s, *, ah, e, dv, ts, tpad, npages, lam_init):
    del pt_ref
    p = pl.program_id(1)
    rph = SUBLANES
    page = kp_ref.shape[2]
    row8 = lax.broadcasted_iota(jnp.int32, (rph, 2 * e), 0)
    lane = lax.broadcasted_iota(jnp.int32, (rph, 2 * e), 1)

    m_rows = ah * rph

    def row_head(shape):
        return lax.shift_right_logical(lax.broadcasted_iota(jnp.int32, shape, 0), int(math.log2(rph)))

    @pl.when(p == 0)
    def _():
        for h in range(ah):
            qh = q_ref[0:rph, h * 2 * e:(h + 1) * 2 * e].astype(F32)
            first = jnp.where((row8 < ts) & (lane < e), qh, 0.0)
            second = jnp.where((row8 >= ts) & (row8 < 2 * ts) & (lane >= e), pltpu.roll(qh, ts, 0), 0.0)
            qs_s[h * rph:(h + 1) * rph] = first + second
        c_head = lax.broadcasted_iota(jnp.int32, bias_s.shape, 1) & (ah - 1)
        bias_s[...] = jnp.where(row_head(bias_s.shape) == c_head, 0.0, NEG_INF)
        m_s[...] = jnp.full(m_s.shape, NEG_INF, F32)
        l_s[...] = jnp.zeros(l_s.shape, F32)
        acc_s[...] = jnp.zeros(acc_s.shape, F32)

    def online_update(s, pv):
        m_prev = m_s[...]
        m_new = jnp.maximum(m_prev, jnp.max(s, axis=-1, keepdims=True))
        alpha = jnp.exp2(m_prev - m_new)
        pr = jnp.exp2(s - m_new)
        l_s[...] = alpha * l_s[...] + jnp.sum(pr, axis=-1, keepdims=True)
        m_s[...] = m_new
        acc_s[...] = alpha * acc_s[...] + pv(pr)

    qs = qs_s[...].astype(BF16)
    kf = kp_ref[0, 0].reshape(page * ah, 2 * e).astype(BF16)
    vf = vp_ref[0, 0].reshape(page * ah, dv).astype(BF16)
    s_page = lax.dot_general(qs, kf, NT_DIMS, preferred_element_type=F32) + bias_s[...]
    online_update(s_page, lambda pr: jnp.dot(pr.astype(BF16), vf, preferred_element_type=F32))

    @pl.when(p == npages - 1)
    def _():
        rh = row_head((m_rows, tpad))
        row = lax.broadcasted_iota(jnp.int32, (m_rows, tpad), 0) & (rph - 1)
        qt = jnp.where(row < ts, row, row - ts)
        kt = lax.broadcasted_iota(jnp.int32, (m_rows, tpad), 1)
        s_new = jnp.full((m_rows, tpad), NEG_INF, F32)
        for h in range(ah):
            kh = kn_ref[:, (ah + h) * 2 * e:(ah + h + 1) * 2 * e]
            sh = lax.dot_general(qs, kh, NT_DIMS, preferred_element_type=F32)
            s_new = jnp.where((rh == h) & (kt <= qt), sh, s_new)

        def pv_new(pr):
            out = jnp.zeros((m_rows, dv), F32)
            for h in range(ah):
                ph = jnp.where(rh == h, pr, 0.0).astype(BF16)
                out = out + jnp.dot(ph, vn_ref[:, h * dv:(h + 1) * dv], preferred_element_type=F32)
            return out

        online_update(s_new, pv_new)

        lam = _lambda_vec(wl_ref, lam_init)
        zeros = jnp.zeros((tpad - rph, dv), F32)
        for h in range(ah):
            rows = slice(h * rph, (h + 1) * rph)
            a = acc_s[rows] * (1.0 / l_s[rows])
            o = a - lam * pltpu.roll(a, rph - ts, 0)
            o = _rms(o, gs_ref[...]) * (1.0 - lam_init)
            o = jnp.where(lax.broadcasted_iota(jnp.int32, o.shape, 0) < ts, o, 0.0)
            o_ref[:, h * dv:(h + 1) * dv] = jnp.concatenate([o, zeros], axis=0).astype(o_ref.dtype)


def _sample_attn(side, cache_k, cache_v, page_table, layer, wl, gs, ts, cfg, lam_init):
    r = side.shape[0]
    e, dv, ah = cfg.e, cfg.dv, cfg.ah
    tpad = SAMPLE_TPAD
    bsz, npages = page_table.shape
    page = cache_k.shape[2]
    assert 2 * ts <= SUBLANES and ah & (ah - 1) == 0
    m_rows = ah * SUBLANES
    grid_spec = pltpu.PrefetchScalarGridSpec(
        num_scalar_prefetch=1,
        grid=(bsz, npages),
        in_specs=[
            pl.BlockSpec((tpad, cfg.qw), lambda b, p, pt: (b, 0)),
            pl.BlockSpec((tpad, 2 * cfg.qw), lambda b, p, pt: (b, 0)),
            pl.BlockSpec((tpad, cfg.vw), lambda b, p, pt: (b, 2 * cfg.qw // cfg.vw)),
            pl.BlockSpec((1, 1, page, ah, 2 * e), lambda b, p, pt: (layer, pt[b * npages + p], 0, 0, 0)),
            pl.BlockSpec((1, 1, page, ah, dv), lambda b, p, pt: (layer, pt[b * npages + p], 0, 0, 0)),
            pl.BlockSpec((4, e), lambda b, p, pt: (0, 0)),
            pl.BlockSpec((1, dv), lambda b, p, pt: (0, 0)),
        ],
        out_specs=pl.BlockSpec((tpad, cfg.vw), lambda b, p, pt: (b, 0)),
        scratch_shapes=[pltpu.VMEM((m_rows, 2 * e), F32), pltpu.VMEM((m_rows, page * ah), F32),
                        pltpu.VMEM((m_rows, 1), F32), pltpu.VMEM((m_rows, 1), F32), pltpu.VMEM((m_rows, dv), F32)],
    )
    return pl.pallas_call(
        functools.partial(_sattn_kernel, ah=ah, e=e, dv=dv, ts=ts, tpad=tpad, npages=npages, lam_init=lam_init),
        grid_spec=grid_spec,
        out_shape=jax.ShapeDtypeStruct((r, cfg.vw), BF16),
        compiler_params=_cparams("parallel", "arbitrary"),
        name="sample_attn",
    )(page_table.reshape(-1), side, side, side, cache_k, cache_v, wl, gs)


def _mlstm_kernel(mq_ref, mk_ref, mv_ref, mo_ref, g_ref, gt_ref, wq_ref, wk_ref, stq_ref, stk_ref,
                  c0_ref, n0_ref, m0_ref, gm_ref, hm_ref, c_ref, n_ref, m_ref, xq_s, xk_s,
                  *, L, dh, mh, conv, t_valid, t_total):
    h = pl.program_id(1)
    c = pl.program_id(2)
    hb = conv - 1
    top = SUBLANES

    @pl.when(c == 0)
    def _():
        xq_s[top - hb:top] = stq_ref[0]
        xk_s[top - hb:top] = stk_ref[0]
        c_ref[0] = c0_ref[0]
        n_ref[0] = n0_ref[0]
        m_ref[0] = m0_ref[0]

    xq_s[top:top + L] = mq_ref[...]
    xk_s[top:top + L] = mk_ref[...]
    yq = _causal_conv(xq_s, wq_ref, top, hb, L)
    yk = _causal_conv(xk_s, wk_ref, top, hb, L)
    q = yq * _sigmoid(yq)
    k = yk * _sigmoid(yk) * dh ** -0.5
    xq_s[top - hb:top] = xq_s[top + L - hb:top + L]
    xk_s[top - hb:top] = xk_s[top + L - hb:top + L]

    g = g_ref[...]
    lane = lax.broadcasted_iota(jnp.int32, g.shape, 1)
    ic_col = jnp.sum(jnp.where(lane == h, g, 0.0), axis=-1, keepdims=True)
    fc_col = jnp.sum(jnp.where(lane == mh + h, g, 0.0), axis=-1, keepdims=True)
    gt = gt_ref[0]
    sub = lax.broadcasted_iota(jnp.int32, gt.shape, 0)
    ic_row = jnp.sum(jnp.where(sub == h, gt, 0.0), axis=0, keepdims=True)
    fc_row = jnp.sum(jnp.where(sub == mh + h, gt, 0.0), axis=0, keepdims=True)
    if t_valid < t_total:
        t_col = c * L + lax.broadcasted_iota(jnp.int32, (L, 1), 0)
        t_row = c * L + lax.broadcasted_iota(jnp.int32, (1, L), 1)
        ic_col = jnp.where(t_col < t_valid, ic_col, NEG_INF)
        fc_col = jnp.where(t_col < t_valid, fc_col, 0.0)
        ic_row = jnp.where(t_row < t_valid, ic_row, NEG_INF)
        fc_row = jnp.where(t_row < t_valid, fc_row, 0.0)

    row = lax.broadcasted_iota(jnp.int32, (L, L), 0)
    col = lax.broadcasted_iota(jnp.int32, (L, L), 1)
    causal = col <= row
    bcum_col = jnp.sum(jnp.where(causal, fc_row, 0.0), axis=-1, keepdims=True)
    bcum_row = jnp.sum(jnp.where(row <= col, fc_col, 0.0), axis=0, keepdims=True)

    m_prev = m_ref[0][:, :1]
    n_prev = n_ref[0]
    c_prev = c_ref[0]

    dmat = jnp.where(causal, bcum_col - bcum_row + ic_row, NEG_INF)
    inter = bcum_col + m_prev
    m_t = jnp.maximum(inter, jnp.max(dmat, axis=-1, keepdims=True))
    w_intra = jnp.exp(dmat - m_t)
    w_inter = jnp.exp(inter - m_t)

    qb = q.astype(BF16)
    kb = k.astype(BF16)
    vb = mv_ref[...].astype(BF16)
    s = lax.dot_general(qb, kb, NT_DIMS, preferred_element_type=F32) * w_intra
    num = (w_inter * jnp.dot(qb, c_prev.astype(BF16), preferred_element_type=F32)
           + jnp.dot(s.astype(BF16), vb, preferred_element_type=F32))
    den = w_inter * jnp.sum(q * n_prev, axis=-1, keepdims=True) + jnp.sum(s, axis=-1, keepdims=True)
    hc = num * (1.0 / jnp.maximum(jnp.abs(den), jnp.exp(-m_t)))

    m_new = m_t[L - 1:L]
    b_last = bcum_col[L - 1:L]
    w_s = jnp.exp(b_last - bcum_col + ic_col - m_new)
    decay = jnp.exp(b_last + m_prev - m_new)
    kw = k * w_s
    c_ref[0] = decay * c_prev + lax.dot_general(kw.astype(BF16), vb, TN_DIMS, preferred_element_type=F32)
    n_ref[0] = decay * n_prev + jnp.sum(kw, axis=0, keepdims=True)
    m_ref[0] = jnp.broadcast_to(m_new, m_ref.shape[1:])

    hm_ref[...] = (_rms(hc, gm_ref[...]) * _sigmoid(mo_ref[...])).astype(hm_ref.dtype)


def _mlstm(z, g, gt, wqk, st_qk, c0, n0, m0, g_mout, bsz, t, t_valid, cfg):
    r = z.shape[0]
    dh, mh, conv = cfg.dh, cfg.mh, cfg.conv
    L = _tile(t, 256, LANES) if t % LANES == 0 else t
    nc = t // L
    o_mq = 0
    bh = bsz * mh
    hb = conv - 1
    m0b = jnp.broadcast_to(m0.reshape(bh, 1, 1), (bh, 1, LANES))
    outs = pl.pallas_call(
        functools.partial(_mlstm_kernel, L=L, dh=dh, mh=mh, conv=conv, t_valid=t_valid, t_total=t),
        grid=(bsz, mh, nc),
        in_specs=[
            pl.BlockSpec((L, dh), lambda b, h, c: (b * nc + c, o_mq + h)),
            pl.BlockSpec((L, dh), lambda b, h, c: (b * nc + c, o_mq + mh + h)),
            pl.BlockSpec((L, dh), lambda b, h, c: (b * nc + c, o_mq + 2 * mh + h)),
            pl.BlockSpec((L, dh), lambda b, h, c: (b * nc + c, o_mq + 3 * mh + h)),
            pl.BlockSpec((L, LANES), lambda b, h, c: (b * nc + c, 0)),
            pl.BlockSpec((1, SUBLANES, L), lambda b, h, c: (b, 0, c)),
            pl.BlockSpec((conv, dh), lambda b, h, c: (0, h)),
            pl.BlockSpec((conv, dh), lambda b, h, c: (0, mh + h)),
            pl.BlockSpec((1, hb, dh), lambda b, h, c: (b, 0, h)),
            pl.BlockSpec((1, hb, dh), lambda b, h, c: (b, 0, mh + h)),
            pl.BlockSpec((1, dh, dh), lambda b, h, c: (b * mh + h, 0, 0)),
            pl.BlockSpec((1, 1, dh), lambda b, h, c: (b * mh + h, 0, 0)),
            pl.BlockSpec((1, 1, LANES), lambda b, h, c: (b * mh + h, 0, 0)),
            pl.BlockSpec((1, dh), lambda b, h, c: (0, h)),
        ],
        out_specs=[
            pl.BlockSpec((L, dh), lambda b, h, c: (b * nc + c, h)),
            pl.BlockSpec((1, dh, dh), lambda b, h, c: (b * mh + h, 0, 0)),
            pl.BlockSpec((1, 1, dh), lambda b, h, c: (b * mh + h, 0, 0)),
            pl.BlockSpec((1, 1, LANES), lambda b, h, c: (b * mh + h, 0, 0)),
        ],
        out_shape=[
            jax.ShapeDtypeStruct((r, cfg.mw), BF16),
            jax.ShapeDtypeStruct((bh, dh, dh), F32),
            jax.ShapeDtypeStruct((bh, 1, dh), F32),
            jax.ShapeDtypeStruct((bh, 1, LANES), F32),
        ],
        scratch_shapes=[pltpu.VMEM((L + SUBLANES, dh), F32), pltpu.VMEM((L + SUBLANES, dh), F32)],
        compiler_params=_cparams("parallel", "parallel", "arbitrary"),
        name="mlstm",
    )(z, z, z, z, g, gt, wqk, wqk, st_qk, st_qk, c0.reshape(bh, dh, dh), n0.reshape(bh, 1, dh), m0b,
      g_mout.reshape(1, cfg.mw))
    hm, c_new, n_new, m_new = outs
    return (hm, c_new.reshape(bsz, mh, dh, dh), n_new.reshape(bsz, mh, dh), m_new[:, 0, 0].reshape(bsz, mh))


def _gated_dual_kernel(ao_ref, hm_ref, ga_ref, gm_ref, wa_ref, wm_ref, o_ref):
    ya = jnp.dot(ao_ref[...], wa_ref[...], preferred_element_type=F32)
    ym = jnp.dot(hm_ref[...], wm_ref[...], preferred_element_type=F32)
    o_ref[...] = (_sigmoid(ga_ref[...]) * ya + _sigmoid(gm_ref[...]) * ym).astype(o_ref.dtype)


def _gated_dual(ao, hm, z, wa, wm, gate_off):
    r = ao.shape[0]
    n = wa.shape[1]
    tm = _tile(r, 512, 2 * SUBLANES)
    tn = _tile(n, 512, LANES)
    ga_b = gate_off // tn
    gm_b = (gate_off + n) // tn
    return pl.pallas_call(
        _gated_dual_kernel,
        grid=(r // tm, n // tn),
        in_specs=[
            pl.BlockSpec((tm, ao.shape[1]), lambda i, j: (i, 0)),
            pl.BlockSpec((tm, hm.shape[1]), lambda i, j: (i, 0)),
            pl.BlockSpec((tm, tn), lambda i, j: (i, ga_b + j)),
            pl.BlockSpec((tm, tn), lambda i, j: (i, gm_b + j)),
            pl.BlockSpec((wa.shape[0], tn), lambda i, j: (0, j)),
            pl.BlockSpec((wm.shape[0], tn), lambda i, j: (0, j)),
        ],
        out_specs=pl.BlockSpec((tm, tn), lambda i, j: (i, j)),
        out_shape=jax.ShapeDtypeStruct((r, n), BF16),
        compiler_params=_cparams("parallel", "arbitrary"),
        name="gated_dual",
    )(ao, hm, z, z, wa, wm)


def _mm_res_kernel(a_ref, w_ref, x_ref, o_ref):
    o_ref[...] = x_ref[...] + jnp.dot(a_ref[...], w_ref[...], preferred_element_type=F32)


def _mm_res(a, w, x, *, name):
    r, k = a.shape
    n = w.shape[1]
    tm = _tile(r, 512, 2 * SUBLANES)
    tn = _tile(n, 512, LANES)
    return pl.pallas_call(
        _mm_res_kernel,
        grid=(r // tm, n // tn),
        in_specs=[
            pl.BlockSpec((tm, k), lambda i, j: (i, 0)),
            pl.BlockSpec((k, tn), lambda i, j: (0, j)),
            pl.BlockSpec((tm, tn), lambda i, j: (i, j)),
        ],
        out_specs=pl.BlockSpec((tm, tn), lambda i, j: (i, j)),
        out_shape=jax.ShapeDtypeStruct((r, n), F32),
        compiler_params=_cparams("parallel", "arbitrary"),
        name=name,
    )(a, w, x)


def _ffn_up_kernel(x_ref, g_ref, wg_ref, wu_ref, st_ref, wc_ref, act_ref, stout_ref, hin_ref, xp_s, tail_s,
                   *, nseq, tp, conv, blocks_per_seq, t_end):
    i = pl.program_id(0)
    j = pl.program_id(1)
    hb = conv - 1
    top = SUBLANES

    @pl.when(j == 0)
    def _():
        hin_ref[...] = _rms(x_ref[...], g_ref[...]).astype(BF16)

    hin = hin_ref[...]
    gate = jnp.dot(hin, wg_ref[...], preferred_element_type=F32)
    up = jnp.dot(hin, wu_ref[...], preferred_element_type=F32)
    for s in range(nseq):
        rows = slice(s * tp, (s + 1) * tp)
        if blocks_per_seq > 1:
            prev = jnp.where((i % blocks_per_seq) == 0, st_ref[s], tail_s[j, top - hb:top])
        else:
            prev = st_ref[s]
        xp_s[top - hb:top] = prev
        xp_s[top:top + tp] = gate[rows]
        y = _causal_conv(xp_s, wc_ref, top, hb, tp)
        act_ref[rows] = (y * _sigmoid(y) * up[rows]).astype(act_ref.dtype)
        stout_ref[0, s] = xp_s[top + t_end - hb:top + t_end]
    if blocks_per_seq > 1:
        tail_s[j, top - hb:top] = xp_s[top + tp - hb:top + tp]


def _ffn_up(x, g, w, st, wc, bsz, t, t_valid, cfg):
    r, k = x.shape
    dff, conv = cfg.dff, cfg.ffconv
    tn = _tile(dff, 512, LANES)
    ncb = dff // tn
    if t >= 256:
        assert t_valid == t
        tm = _tile(t, 1024, 2 * SUBLANES)
        nseq, tp, bps, t_end = 1, tm, t // tm, tm
    else:
        tm, nseq, tp, bps, t_end = r, bsz, t, 1, t_valid
    nrb = r // tm
    act, tails = pl.pallas_call(
        functools.partial(_ffn_up_kernel, nseq=nseq, tp=tp, conv=conv, blocks_per_seq=bps, t_end=t_end),
        grid=(nrb, ncb),
        in_specs=[
            pl.BlockSpec((tm, k), lambda i, j: (i, 0)),
            pl.BlockSpec((1, k), lambda i, j: (0, 0)),
            pl.BlockSpec((k, tn), lambda i, j: (0, j)),
            pl.BlockSpec((k, tn), lambda i, j: (0, ncb + j)),
            pl.BlockSpec((nseq, conv - 1, tn), lambda i, j: (i // bps if nseq == 1 else 0, 0, j)),
            pl.BlockSpec((conv, tn), lambda i, j: (0, j)),
        ],
        out_specs=[
            pl.BlockSpec((tm, tn), lambda i, j: (i, j)),
            pl.BlockSpec((1, nseq, conv - 1, tn), lambda i, j: (i, 0, 0, j)),
        ],
        out_shape=[jax.ShapeDtypeStruct((r, dff), BF16),
                   jax.ShapeDtypeStruct((nrb, nseq, conv - 1, dff), F32)],
        scratch_shapes=[pltpu.VMEM((tm, k), BF16), pltpu.VMEM((tp + SUBLANES, tn), F32),
                        pltpu.VMEM((ncb, SUBLANES, tn), F32)],
        compiler_params=_cparams("arbitrary", "arbitrary"),
        name="ffn_up",
    )(x, g, w, w, st, wc)
    if nseq == 1:
        return act, tails.reshape(bsz, bps, conv - 1, dff)[:, bps - 1]
    return act, tails[0]


def _layer(x, bsz, t, t_valid, layer, depth, lw, kv_all, st_qk, st_ff, c0, n0, m0, attn, cfg):
    mw = cfg.mw
    lam_init = 0.8 - 0.6 * math.exp(-0.3 * layer)

    z, side, k_all, v_all = _in_proj(x, lw['g_mix'], lw['w_a'], lw['w_g'], lw['g_qk'], kv_all, layer, depth, cfg)
    g, gt = _gates(x, lw['g_mix'], lw['w_if'], lw['b_if'], bsz, t, cfg)
    ao = attn(side, lw['w_lambda'], lw['g_sub'], lam_init)
    hm, c_new, n_new, m_new = _mlstm(z, g, gt, lw['w_qkconv'], st_qk, c0, n0, m0, lw['g_mout'], bsz, t, t_valid, cfg)
    merged = _gated_dual(ao, hm, z, lw['w_br_attn'], lw['w_br_mlstm'], 4 * mw)
    x1 = _mm_res(merged, lw['w_out'], x, name="out_proj")
    act, ff_new = _ffn_up(x1, lw['g_ffn'], lw['w_ff_in'], st_ff, lw['w_ffconv'], bsz, t, t_valid, cfg)
    x2 = _mm_res(act, lw['w_ff_out'], x1, name="ffn_down")

    qk_new = z.reshape(bsz, t, -1)[:, t_valid - (cfg.conv - 1):t_valid, :2 * mw]
    return x2, (k_all, v_all), (c_new, n_new, m_new, qk_new, ff_new)


def _layer_weights(l, cfg, g_mix, w_in, g_q, g_k, w_lambda, g_sub, w_qkconv, b_if, g_mout, w_br_attn,
                   w_br_mlstm, w_out, g_ffn, w_ff_in, w_ffconv, w_ff_out):
    qw, vw, mw, d = cfg.qw, cfg.vw, cfg.mw, cfg.d
    o_if = 2 * qw + vw + 4 * mw
    ng = 2 * cfg.mh
    w = w_in[l]
    reps = qw // cfg.e
    return {
        'g_mix': g_mix[l].reshape(1, d),
        'w_a': w[:, :o_if].astype(BF16),
        'w_g': w[:, o_if + ng:].astype(BF16),
        'w_if': w[:, o_if:o_if + ng],
        'b_if': b_if[l].reshape(ng),
        'g_qk': jnp.concatenate([jnp.tile(g_q[l], reps), jnp.tile(g_k[l], reps)]).reshape(1, 2 * qw),
        'w_lambda': w_lambda[l],
        'g_sub': g_sub[l].reshape(1, cfg.dv),
        'w_qkconv': w_qkconv[l],
        'g_mout': g_mout[l],
        'w_br_attn': w_br_attn[l].astype(BF16),
        'w_br_mlstm': w_br_mlstm[l].astype(BF16),
        'w_out': w_out[l].astype(BF16),
        'g_ffn': g_ffn[l].reshape(1, d),
        'w_ff_in': w_ff_in[l].astype(BF16),
        'w_ffconv': w_ffconv[l],
        'w_ff_out': w_ff_out[l].astype(BF16),
    }


def kernel(x_prompt, x_sample, cache_k, cache_v, page_table, state_C, state_n, state_m, state_qkconv,
           state_ffconv, g_mix, w_in, g_q, g_k, w_lambda, g_sub, w_qkconv, b_if, g_mout, w_br_attn,
           w_br_mlstm, w_out, g_ffn, w_ff_in, w_ffconv, w_ff_out):
    bp, tp_, d = x_prompt.shape
    bs, ts, _ = x_sample.shape
    depth = w_in.shape[0]
    cfg = Cfg(d=d, ah=cache_k.shape[3], e=cache_k.shape[4] // 2, dv=cache_v.shape[4], mh=state_C.shape[2],
              dh=state_C.shape[3], conv=w_qkconv.shape[1], dff=w_ff_out.shape[1], ffconv=w_ffconv.shape[1])
    assert ts <= SAMPLE_TPAD and ts >= max(cfg.conv, cfg.ffconv) - 1

    yp = x_prompt.reshape(bp * tp_, d)
    ys = jnp.zeros((bs, SAMPLE_TPAD, d), F32).at[:, :ts].set(x_sample).reshape(bs * SAMPLE_TPAD, d)
    zero_qk = jnp.zeros((bp, cfg.conv - 1, 2 * cfg.mw), F32)
    zero_ff = jnp.zeros((bp, cfg.ffconv - 1, cfg.dff), F32)
    zero_c = jnp.zeros((bp, cfg.mh, cfg.dh, cfg.dh), F32)
    zero_n = jnp.zeros((bp, cfg.mh, cfg.dh), F32)
    zero_m = jnp.zeros((bp, cfg.mh), F32)

    new_p = [[] for _ in range(5)]
    new_s = [[] for _ in range(5)]
    kv_p = kv_s = None
    for l in range(depth):
        lw = _layer_weights(l, cfg, g_mix, w_in, g_q, g_k, w_lambda, g_sub, w_qkconv, b_if, g_mout,
                            w_br_attn, w_br_mlstm, w_out, g_ffn, w_ff_in, w_ffconv, w_ff_out)
        p_attn = functools.partial(_prompt_attn, bsz=bp, t=tp_, cfg=cfg)
        yp, kv_p, sp = _layer(yp, bp, tp_, tp_, l, depth, lw, kv_p, zero_qk, zero_ff, zero_c, zero_n, zero_m,
                              lambda side, wl, gs, li: p_attn(side, wl, gs, lam_init=li), cfg)
        s_attn = functools.partial(_sample_attn, cache_k=cache_k, cache_v=cache_v, page_table=page_table,
                                   layer=l, ts=ts, cfg=cfg)
        ys, kv_s, ss = _layer(ys, bs, SAMPLE_TPAD, ts, l, depth, lw, kv_s, state_qkconv[l], state_ffconv[l],
                              state_C[l], state_n[l], state_m[l],
                              lambda side, wl, gs, li: s_attn(side, wl=wl, gs=gs, lam_init=li), cfg)
        for j in range(5):
            new_p[j].append(sp[j])
            new_s[j].append(ss[j])
    outs_p = [jnp.stack(a, axis=0) for a in new_p]
    outs_s = [jnp.stack(a, axis=0) for a in new_s]
    kp = kv_p[0].reshape(depth, bp, tp_, cfg.ah, 2 * cfg.e)
    vp = kv_p[1].reshape(depth, bp, tp_, cfg.ah, cfg.dv)
    ks = kv_s[0].reshape(depth, bs, SAMPLE_TPAD, cfg.ah, 2 * cfg.e)[:, :, :ts]
    vs = kv_s[1].reshape(depth, bs, SAMPLE_TPAD, cfg.ah, cfg.dv)[:, :, :ts]
    y_prompt = yp.reshape(bp, tp_, d)
    y_sample = ys.reshape(bs, SAMPLE_TPAD, d)[:, :ts]
    return (y_prompt, y_sample, kp, vp, *outs_p, ks, vs, *outs_s)
```

```python
import functools
import math
from typing import NamedTuple

import numpy as np
import jax
import jax.numpy as jnp
from jax import lax
from jax.experimental import pallas as pl
from jax.experimental.pallas import tpu as pltpu

F32 = jnp.float32
BF16 = jnp.bfloat16
EPS = 1e-6
NEG_INF = float("-inf")
LOG2E = math.log2(math.e)

LANES = 128
SUBLANES = 8
SAMPLE_TPAD = 16
VMEM_LIMIT = 52 * 2 ** 20

NT_DIMS = (((1,), (1,)), ((), ()))
NN_DIMS = (((1,), (0,)), ((), ()))
TN_DIMS = (((0,), (0,)), ((), ()))


class Cfg(NamedTuple):
    d: int
    ah: int
    e: int
    dv: int
    mh: int
    dh: int
    conv: int
    dff: int
    ffconv: int

    @property
    def qw(self):
        return self.ah * 2 * self.e

    @property
    def vw(self):
        return self.ah * self.dv

    @property
    def mw(self):
        return self.mh * self.dh


def _tile(n, pref, mult):
    t = min(pref, n)
    t -= t % mult
    while t >= mult:
        if n % t == 0:
            return t
        t -= mult
    return n


def _cparams(*sem):
    return pltpu.CompilerParams(dimension_semantics=sem, vmem_limit_bytes=VMEM_LIMIT)


def _sigmoid(x):
    return 1.0 / (1.0 + jnp.exp(-x))


def _log_sigmoid(x):
    return -(jnp.maximum(-x, 0.0) + jnp.log1p(jnp.exp(-jnp.abs(x))))


def _rms(x, g):
    ms = jnp.mean(x * x, axis=-1, keepdims=True)
    return x * lax.rsqrt(ms + EPS) * g


def _split_bf16(a):
    hi = a.astype(BF16)
    lo = (a - hi.astype(F32)).astype(BF16)
    return hi, lo


def _dot3(a, b, dims):
    ah, al = _split_bf16(a)
    bh, bl = _split_bf16(b)
    f = lambda p, q: lax.dot_general(p, q, dims, preferred_element_type=F32)
    return f(ah, bh) + (f(ah, bl) + f(al, bh))


def _weights_bf16(w_ref):
    w = w_ref[0] if len(w_ref.shape) == 3 else w_ref[...]
    return w.astype(BF16)


def _layer_block(layer, k, tn, col_of):
    return pl.BlockSpec((1, k, tn), lambda i, j: (layer, 0, col_of(j)))


def _causal_conv(xp_s, w_ref, top, hb, rows):
    y = None
    for j in range(hb + 1):
        term = xp_s[top - hb + j:top - hb + j + rows] * w_ref[j:j + 1]
        y = term if y is None else y + term
    return y


def _in_proj_kernel(*refs, n_in, emit_vt, nqb, nvb, na, q_scale, tn):
    x_ref, g_ref, wa_ref, wg_ref, gn_ref = refs[:5]
    z_ref, side_ref, k_ref, v_ref = refs[n_in:n_in + 4]
    vt_ref = refs[n_in + 4] if emit_vt else None
    hin_ref = refs[-1]
    j = pl.program_id(1)
    jv0 = 2 * nqb
    j0 = jv0 + nvb

    @pl.when(j == 0)
    def _():
        hin_ref[...] = _rms(x_ref[...], g_ref[...]).astype(BF16)

    def proj(w_ref):
        return jnp.dot(hin_ref[...], _weights_bf16(w_ref), preferred_element_type=F32)

    def qk_norm(side_scale, f32_ref):
        acc = proj(wa_ref)
        for c in range(tn // LANES):
            sl = slice(c * LANES, (c + 1) * LANES)
            y = _rms(acc[:, sl], gn_ref[:, sl])
            if f32_ref is not None:
                f32_ref[:, sl] = y
            side_ref[:, sl] = (y if side_scale is None else y * side_scale).astype(BF16)

    @pl.when(j < nqb)
    def _():
        qk_norm(q_scale, None)

    @pl.when((j >= nqb) & (j < jv0))
    def _():
        qk_norm(None, k_ref)

    @pl.when((j >= jv0) & (j < j0))
    def _():
        acc = proj(wa_ref)
        v_ref[...] = acc
        side_ref[...] = acc.astype(BF16)
        if emit_vt:
            vt_ref[...] = acc.T.astype(BF16)

    @pl.when((j >= j0) & (j < na))
    def _():
        z_ref[...] = proj(wa_ref)

    @pl.when(j >= na)
    def _():
        z_ref[...] = proj(wg_ref)


def _in_proj(x, g, w_in, n_a, wg, gn, kv_all, layer, depth, cfg, t_seq=None):
    r, k = x.shape
    qw, vw = cfg.qw, cfg.vw
    n_side = 2 * qw + vw
    n_z = n_a - n_side + wg.shape[1]
    tm = _tile(r, 1024, SUBLANES)
    tn = _tile(functools.reduce(math.gcd, (qw, vw, n_a, wg.shape[1])), 512, LANES)
    nqb, nvb, na, nsb = qw // tn, vw // tn, n_a // tn, n_side // tn
    nrb = r // tm
    in_specs = [
        pl.BlockSpec((tm, k), lambda i, j: (i, 0)),
        pl.BlockSpec((1, k), lambda i, j: (0, 0)),
        _layer_block(layer, k, tn, lambda j: jnp.minimum(j, na - 1)),
        pl.BlockSpec((k, tn), lambda i, j: (0, jnp.maximum(j - na, 0))),
        pl.BlockSpec((1, tn), lambda i, j: (0, jnp.minimum(j, 2 * nqb - 1))),
    ]
    args = [x, g, w_in, wg, gn]
    aliases = {}
    if kv_all is not None:
        in_specs += [pl.BlockSpec(memory_space=pl.ANY), pl.BlockSpec(memory_space=pl.ANY)]
        args += list(kv_all)
        aliases = {5: 2, 6: 3}
    out_specs = [
        pl.BlockSpec((tm, tn), lambda i, j: (i, jnp.maximum(j - nsb, 0))),
        pl.BlockSpec((tm, tn), lambda i, j: (i, jnp.minimum(j, nsb - 1))),
        pl.BlockSpec((tm, tn), lambda i, j: (layer * nrb + i, jnp.clip(j - nqb, 0, nqb - 1))),
        pl.BlockSpec((tm, tn), lambda i, j: (layer * nrb + i, jnp.clip(j - 2 * nqb, 0, nvb - 1))),
    ]
    out_shape = [jax.ShapeDtypeStruct((r, n_z), F32), jax.ShapeDtypeStruct((r, n_side), BF16),
                 jax.ShapeDtypeStruct((depth * r, qw), F32), jax.ShapeDtypeStruct((depth * r, vw), F32)]
    emit_vt = t_seq is not None and t_seq % tm == 0
    if emit_vt:
        tpb = t_seq // tm
        out_specs.append(pl.BlockSpec(
            (tn, tm), lambda i, j: ((i // tpb) * nvb + jnp.clip(j - 2 * nqb, 0, nvb - 1), i % tpb)))
        out_shape.append(jax.ShapeDtypeStruct((r // t_seq * vw, t_seq), BF16))
    return pl.pallas_call(
        functools.partial(_in_proj_kernel, n_in=len(args), emit_vt=emit_vt, nqb=nqb, nvb=nvb, na=na,
                          q_scale=cfg.e ** -0.5 * LOG2E, tn=tn),
        grid=(nrb, na + wg.shape[1] // tn),
        in_specs=in_specs,
        out_specs=out_specs,
        out_shape=out_shape,
        scratch_shapes=[pltpu.VMEM((tm, k), BF16)],
        input_output_aliases=aliases,
        compiler_params=_cparams("parallel", "arbitrary"),
        name="in_proj",
    )(*args)


def _gates_kernel(x_ref, g_ref, w_ref, wt_ref, b_ref, bt_ref, o_ref, ot_ref, *, mh):
    hin = _rms(x_ref[...], g_ref[...])
    gif = _dot3(hin, w_ref[...], NN_DIMS) + b_ref[...]
    lane = lax.broadcasted_iota(jnp.int32, gif.shape, 1)
    o_ref[...] = jnp.where(lane < mh, gif, _log_sigmoid(gif))
    gift = _dot3(wt_ref[...], hin, NT_DIMS) + bt_ref[:, :1]
    sub = lax.broadcasted_iota(jnp.int32, gift.shape, 0)
    ot_ref[0] = jnp.where(sub < mh, gift, _log_sigmoid(gift))


def _gates(x, g, wif, b_if, bsz, t, cfg):
    r, k = x.shape
    ng = 2 * cfg.mh
    assert ng <= SUBLANES
    tm = _tile(t, 512, LANES) if t % LANES == 0 else t
    w = jnp.zeros((k, LANES), F32).at[:, :ng].set(wif)
    wt = jnp.zeros((SUBLANES, k), F32).at[:ng].set(wif.T)
    b = jnp.zeros((1, LANES), F32).at[0, :ng].set(b_if)
    bt = jnp.zeros((SUBLANES, LANES), F32).at[:ng].set(jnp.broadcast_to(b_if[:, None], (ng, LANES)))
    nt = t // tm
    return pl.pallas_call(
        functools.partial(_gates_kernel, mh=cfg.mh),
        grid=(bsz, nt),
        in_specs=[
            pl.BlockSpec((tm, k), lambda bi, ti: (bi * nt + ti, 0)),
            pl.BlockSpec((1, k), lambda bi, ti: (0, 0)),
            pl.BlockSpec((k, LANES), lambda bi, ti: (0, 0)),
            pl.BlockSpec((SUBLANES, k), lambda bi, ti: (0, 0)),
            pl.BlockSpec((1, LANES), lambda bi, ti: (0, 0)),
            pl.BlockSpec((SUBLANES, LANES), lambda bi, ti: (0, 0)),
        ],
        out_specs=[
            pl.BlockSpec((tm, LANES), lambda bi, ti: (bi * nt + ti, 0)),
            pl.BlockSpec((1, SUBLANES, tm), lambda bi, ti: (bi, 0, ti)),
        ],
        out_shape=[jax.ShapeDtypeStruct((r, LANES), F32),
                   jax.ShapeDtypeStruct((bsz, SUBLANES, t), F32)],
        compiler_params=_cparams("parallel", "parallel"),
        name="gates",
    )(x, g, w, wt, b, bt)


def _lambda_vec(wl_ref, lam_init):
    wl = wl_ref[...]
    a = jnp.sum(wl[0:1] * wl[1:2], axis=-1, keepdims=True)
    b = jnp.sum(wl[2:3] * wl[3:4], axis=-1, keepdims=True)
    return jnp.exp(a) - jnp.exp(b) + lam_init


def _pattn_kernel(qi_ref, ki_ref, q_ref, k_ref, vt_ref, wl_ref, gs_ref, o_ref, m_s, l_s, acc_s,
                  *, tq, e, dv, hpb, lam_init):
    step = pl.program_id(2)
    qi = qi_ref[step]
    ki = ki_ref[step]

    @pl.when(ki == 0)
    def _():
        m_s[...] = jnp.full(m_s.shape, NEG_INF, F32)
        l_s[...] = jnp.zeros(l_s.shape, F32)
        acc_s[...] = jnp.zeros(acc_s.shape, F32)

    def body(masked):
        if masked:
            key = lax.broadcasted_iota(jnp.int32, (tq, tq), 0)
            qry = lax.broadcasted_iota(jnp.int32, (tq, tq), 1)
            keep = key <= qry
        for hc in range(2 * hpb):
            cols = slice(hc * e, (hc + 1) * e)
            st = lax.dot_general(k_ref[:, cols], q_ref[:, cols], NT_DIMS,
                                 preferred_element_type=F32)
            if masked:
                st = jnp.where(keep, st, NEG_INF)
            m_prev = m_s[hc]
            m_new = jnp.maximum(m_prev, jnp.max(st, axis=0, keepdims=True))
            alpha = jnp.exp2(m_prev - m_new)
            pt = jnp.exp2(st - m_new)
            l_s[hc] = alpha * l_s[hc] + jnp.sum(pt, axis=0, keepdims=True)
            m_s[hc] = m_new
            vt = vt_ref[(hc // 2) * dv:(hc // 2 + 1) * dv]
            acc_s[hc] = alpha * acc_s[hc] + jnp.dot(vt, pt.astype(BF16), preferred_element_type=F32)

    @pl.when(ki < qi)
    def _():
        body(False)

    @pl.when(ki == qi)
    def _():
        body(True)
        lam = _lambda_vec(wl_ref, lam_init)
        for h in range(hpb):
            ot = (acc_s[2 * h] * (1.0 / l_s[2 * h])
                  - lam * (acc_s[2 * h + 1] * (1.0 / l_s[2 * h + 1])))
            ms = jnp.mean(ot * ot, axis=0, keepdims=True)
            ot = ot * lax.rsqrt(ms + EPS) * gs_ref[...] * (1.0 - lam_init)
            o_ref[:, h * dv:(h + 1) * dv] = ot.T.astype(o_ref.dtype)


def _prompt_attn(side, vt, wl, gs, bsz, t, cfg, lam_init):
    r = side.shape[0]
    e, dv, ah = cfg.e, cfg.dv, cfg.ah
    tq = _tile(t, 512, LANES)
    nq = t // tq
    kb = cfg.qw // (2 * e)
    pairs = [(qi, ki) for qi in range(nq) for ki in range(qi + 1)]
    qi_tab = jnp.asarray(np.array([p[0] for p in pairs], np.int32))
    ki_tab = jnp.asarray(np.array([p[1] for p in pairs], np.int32))
    if vt is None:
        vt = side[:, 2 * cfg.qw:].reshape(bsz, t, ah, dv).transpose(0, 2, 3, 1).reshape(bsz * ah * dv, t)
    hpb = 2 if ah % 2 == 0 else 1
    nhb = ah // hpb
    kb = kb // hpb
    grid_spec = pltpu.PrefetchScalarGridSpec(
        num_scalar_prefetch=2,
        grid=(bsz, nhb, len(pairs)),
        in_specs=[
            pl.BlockSpec((tq, hpb * 2 * e), lambda b, h, s, qt, kt: (b * nq + qt[s], h)),
            pl.BlockSpec((tq, hpb * 2 * e), lambda b, h, s, qt, kt: (b * nq + kt[s], kb + h)),
            pl.BlockSpec((hpb * dv, tq), lambda b, h, s, qt, kt: (b * nhb + h, kt[s])),
            pl.BlockSpec((4, e), lambda b, h, s, qt, kt: (0, 0)),
            pl.BlockSpec((dv, 1), lambda b, h, s, qt, kt: (0, 0)),
        ],
        out_specs=pl.BlockSpec((tq, hpb * dv), lambda b, h, s, qt, kt: (b * nq + qt[s], h)),
        scratch_shapes=[pltpu.VMEM((2 * hpb, 1, tq), F32), pltpu.VMEM((2 * hpb, 1, tq), F32),
                        pltpu.VMEM((2 * hpb, dv, tq), F32)],
    )
    return pl.pallas_call(
        functools.partial(_pattn_kernel, tq=tq, e=e, dv=dv, hpb=hpb, lam_init=lam_init),
        grid_spec=grid_spec,
        out_shape=jax.ShapeDtypeStruct((r, cfg.vw), BF16),
        compiler_params=_cparams("parallel", "parallel", "arbitrary"),
        name="prompt_attn",
    )(qi_tab, ki_tab, side, side, vt, wl, gs.reshape(dv, 1))


def _sattn_kernel(pt_ref, q_ref, kn_ref, vn_ref, *refs, ah, e, dv, ts, tpad, nsteps, ppb, lam_init):
    del pt_ref
    kp_refs, vp_refs = refs[:ppb], refs[ppb:2 * ppb]
    wl_ref, gs_ref, o_ref, qs_s, bias_s, m_s, l_s, acc_s = refs[2 * ppb:]
    p = pl.program_id(1)
    rph = SUBLANES
    page = kp_refs[0].shape[2]
    row8 = lax.broadcasted_iota(jnp.int32, (rph, 2 * e), 0)
    lane = lax.broadcasted_iota(jnp.int32, (rph, 2 * e), 1)

    m_rows = ah * rph

    def row_head(shape):
        return lax.shift_right_logical(lax.broadcasted_iota(jnp.int32, shape, 0), int(math.log2(rph)))

    @pl.when(p == 0)
    def _():
        for h in range(ah):
            qh = q_ref[0:rph, h * 2 * e:(h + 1) * 2 * e].astype(F32)
            first = jnp.where((row8 < ts) & (lane < e), qh, 0.0)
            second = jnp.where((row8 >= ts) & (row8 < 2 * ts) & (lane >= e), pltpu.roll(qh, ts, 0), 0.0)
            qs_s[h * rph:(h + 1) * rph] = first + second
        c_head = lax.broadcasted_iota(jnp.int32, bias_s.shape, 1) & (ah - 1)
        bias_s[...] = jnp.where(row_head(bias_s.shape) == c_head, 0.0, NEG_INF)
        m_s[...] = jnp.full(m_s.shape, NEG_INF, F32)
        l_s[...] = jnp.zeros(l_s.shape, F32)
        acc_s[...] = jnp.zeros(acc_s.shape, F32)

    def online_update(scores, pv):
        m_prev = m_s[...]
        m_new = m_prev
        for s in scores:
            m_new = jnp.maximum(m_new, jnp.max(s, axis=-1, keepdims=True))
        alpha = jnp.exp2(m_prev - m_new)
        l_new = alpha * l_s[...]
        acc = alpha * acc_s[...]
        for i, s in enumerate(scores):
            pr = jnp.exp2(s - m_new)
            l_new = l_new + jnp.sum(pr, axis=-1, keepdims=True)
            acc = acc + pv(i, pr)
        l_s[...] = l_new
        m_s[...] = m_new
        acc_s[...] = acc

    qs = qs_s[...].astype(BF16)
    bias = bias_s[...]
    scores = [lax.dot_general(qs, kp_refs[i][0, 0].reshape(page * ah, 2 * e).astype(BF16), NT_DIMS,
                              preferred_element_type=F32) + bias for i in range(ppb)]
    online_update(scores, lambda i, pr: jnp.dot(pr.astype(BF16), vp_refs[i][0, 0].reshape(page * ah, dv).astype(BF16),
                                                preferred_element_type=F32))

    @pl.when(p == nsteps - 1)
    def _():
        rh = row_head((m_rows, tpad))
        row = lax.broadcasted_iota(jnp.int32, (m_rows, tpad), 0) & (rph - 1)
        qt = jnp.where(row < ts, row, row - ts)
        kt = lax.broadcasted_iota(jnp.int32, (m_rows, tpad), 1)
        s_new = jnp.full((m_rows, tpad), NEG_INF, F32)
        for h in range(ah):
            kh = kn_ref[:, (ah + h) * 2 * e:(ah + h + 1) * 2 * e]
            sh = lax.dot_general(qs, kh, NT_DIMS, preferred_element_type=F32)
            s_new = jnp.where((rh == h) & (kt <= qt), sh, s_new)

        def pv_new(_, pr):
            out = jnp.zeros((m_rows, dv), F32)
            for h in range(ah):
                ph = jnp.where(rh == h, pr, 0.0).astype(BF16)
                out = out + jnp.dot(ph, vn_ref[:, h * dv:(h + 1) * dv], preferred_element_type=F32)
            return out

        online_update([s_new], pv_new)

        lam = _lambda_vec(wl_ref, lam_init)
        zeros = jnp.zeros((tpad - rph, dv), F32)
        for h in range(ah):
            rows = slice(h * rph, (h + 1) * rph)
            a = acc_s[rows] * (1.0 / l_s[rows])
            o = a - lam * pltpu.roll(a, rph - ts, 0)
            o = _rms(o, gs_ref[...]) * (1.0 - lam_init)
            o = jnp.where(lax.broadcasted_iota(jnp.int32, o.shape, 0) < ts, o, 0.0)
            o_ref[:, h * dv:(h + 1) * dv] = jnp.concatenate([o, zeros], axis=0).astype(o_ref.dtype)


def _sample_attn(side, cache_k, cache_v, page_table, layer, wl, gs, ts, cfg, lam_init):
    r = side.shape[0]
    e, dv, ah = cfg.e, cfg.dv, cfg.ah
    tpad = SAMPLE_TPAD
    bsz, npages = page_table.shape
    page = cache_k.shape[2]
    assert 2 * ts <= SUBLANES and ah & (ah - 1) == 0
    m_rows = ah * SUBLANES
    ppb = _tile(npages, 4, 1)

    def page_spec(width, i):
        return pl.BlockSpec((1, 1, page, ah, width),
                            lambda b, p, pt: (layer, pt[b * npages + p * ppb + i], 0, 0, 0))

    grid_spec = pltpu.PrefetchScalarGridSpec(
        num_scalar_prefetch=1,
        grid=(bsz, npages // ppb),
        in_specs=[
            pl.BlockSpec((tpad, cfg.qw), lambda b, p, pt: (b, 0)),
            pl.BlockSpec((tpad, 2 * cfg.qw), lambda b, p, pt: (b, 0)),
            pl.BlockSpec((tpad, cfg.vw), lambda b, p, pt: (b, 2 * cfg.qw // cfg.vw)),
            *[page_spec(2 * e, i) for i in range(ppb)],
            *[page_spec(dv, i) for i in range(ppb)],
            pl.BlockSpec((4, e), lambda b, p, pt: (0, 0)),
            pl.BlockSpec((1, dv), lambda b, p, pt: (0, 0)),
        ],
        out_specs=pl.BlockSpec((tpad, cfg.vw), lambda b, p, pt: (b, 0)),
        scratch_shapes=[pltpu.VMEM((m_rows, 2 * e), F32), pltpu.VMEM((m_rows, page * ah), F32),
                        pltpu.VMEM((m_rows, 1), F32), pltpu.VMEM((m_rows, 1), F32), pltpu.VMEM((m_rows, dv), F32)],
    )
    return pl.pallas_call(
        functools.partial(_sattn_kernel, ah=ah, e=e, dv=dv, ts=ts, tpad=tpad, nsteps=npages // ppb, ppb=ppb,
                          lam_init=lam_init),
        grid_spec=grid_spec,
        out_shape=jax.ShapeDtypeStruct((r, cfg.vw), BF16),
        compiler_params=_cparams("parallel", "arbitrary"),
        name="sample_attn",
    )(page_table.reshape(-1), side, side, side, *([cache_k] * ppb), *([cache_v] * ppb), wl, gs)


def _mlstm_kernel(mq_ref, mk_ref, mv_ref, mo_ref, g_ref, gt_ref, wq_ref, wk_ref, stq_ref, stk_ref,
                  c0_ref, n0_ref, m0_ref, gm_ref, hm_ref, c_ref, n_ref, m_ref, xq_s, xk_s,
                  *, L, dh, mh, conv, t_valid, t_total):
    h = pl.program_id(1)
    c = pl.program_id(2)
    hb = conv - 1
    top = SUBLANES

    @pl.when(c == 0)
    def _():
        xq_s[top - hb:top] = stq_ref[0]
        xk_s[top - hb:top] = stk_ref[0]
        c_ref[0] = c0_ref[0]
        n_ref[0] = n0_ref[0]
        m_ref[0] = m0_ref[0]

    xq_s[top:top + L] = mq_ref[...]
    xk_s[top:top + L] = mk_ref[...]
    yq = _causal_conv(xq_s, wq_ref, top, hb, L)
    yk = _causal_conv(xk_s, wk_ref, top, hb, L)
    q = yq * _sigmoid(yq)
    k = yk * _sigmoid(yk) * dh ** -0.5
    xq_s[top - hb:top] = xq_s[top + L - hb:top + L]
    xk_s[top - hb:top] = xk_s[top + L - hb:top + L]

    g = g_ref[...]
    lane = lax.broadcasted_iota(jnp.int32, g.shape, 1)
    ic_col = jnp.sum(jnp.where(lane == h, g, 0.0), axis=-1, keepdims=True)
    fc_col = jnp.sum(jnp.where(lane == mh + h, g, 0.0), axis=-1, keepdims=True)
    gt = gt_ref[0]
    sub = lax.broadcasted_iota(jnp.int32, gt.shape, 0)
    ic_row = jnp.sum(jnp.where(sub == h, gt, 0.0), axis=0, keepdims=True)
    fc_row = jnp.sum(jnp.where(sub == mh + h, gt, 0.0), axis=0, keepdims=True)
    if t_valid < t_total:
        t_col = c * L + lax.broadcasted_iota(jnp.int32, (L, 1), 0)
        t_row = c * L + lax.broadcasted_iota(jnp.int32, (1, L), 1)
        ic_col = jnp.where(t_col < t_valid, ic_col, NEG_INF)
        fc_col = jnp.where(t_col < t_valid, fc_col, 0.0)
        ic_row = jnp.where(t_row < t_valid, ic_row, NEG_INF)
        fc_row = jnp.where(t_row < t_valid, fc_row, 0.0)

    row = lax.broadcasted_iota(jnp.int32, (L, L), 0)
    col = lax.broadcasted_iota(jnp.int32, (L, L), 1)
    causal = col <= row
    bcum_col = jnp.sum(jnp.where(causal, fc_row, 0.0), axis=-1, keepdims=True)
    bcum_row = jnp.sum(jnp.where(row <= col, fc_col, 0.0), axis=0, keepdims=True)

    m_prev = m_ref[0][:, :1]
    n_prev = n_ref[0]
    c_prev = c_ref[0]

    dmat = jnp.where(causal, bcum_col - bcum_row + ic_row, NEG_INF)
    inter = bcum_col + m_prev
    m_t = jnp.maximum(inter, jnp.max(dmat, axis=-1, keepdims=True))
    w_intra = jnp.exp(dmat - m_t)
    w_inter = jnp.exp(inter - m_t)

    qb = q.astype(BF16)
    kb = k.astype(BF16)
    vb = mv_ref[...].astype(BF16)
    s = lax.dot_general(qb, kb, NT_DIMS, preferred_element_type=F32) * w_intra
    num = (w_inter * jnp.dot(qb, c_prev.astype(BF16), preferred_element_type=F32)
           + jnp.dot(s.astype(BF16), vb, preferred_element_type=F32))
    den = w_inter * jnp.sum(q * n_prev, axis=-1, keepdims=True) + jnp.sum(s, axis=-1, keepdims=True)
    hc = num * (1.0 / jnp.maximum(jnp.abs(den), jnp.exp(-m_t)))

    m_new = m_t[L - 1:L]
    b_last = bcum_col[L - 1:L]
    w_s = jnp.exp(b_last - bcum_col + ic_col - m_new)
    decay = jnp.exp(b_last + m_prev - m_new)
    kw = k * w_s
    c_ref[0] = decay * c_prev + lax.dot_general(kw.astype(BF16), vb, TN_DIMS, preferred_element_type=F32)
    n_ref[0] = decay * n_prev + jnp.sum(kw, axis=0, keepdims=True)
    m_ref[0] = jnp.broadcast_to(m_new, m_ref.shape[1:])

    hm_ref[...] = (_rms(hc, gm_ref[...]) * _sigmoid(mo_ref[...])).astype(hm_ref.dtype)


def _mlstm(z, g, gt, wqk, st_qk, c0, n0, m0, g_mout, bsz, t, t_valid, cfg):
    r = z.shape[0]
    dh, mh, conv = cfg.dh, cfg.mh, cfg.conv
    L = _tile(t, 256, LANES) if t % LANES == 0 else t
    nc = t // L
    o_mq = 0
    bh = bsz * mh
    hb = conv - 1
    m0b = jnp.broadcast_to(m0.reshape(bh, 1, 1), (bh, 1, LANES))
    outs = pl.pallas_call(
        functools.partial(_mlstm_kernel, L=L, dh=dh, mh=mh, conv=conv, t_valid=t_valid, t_total=t),
        grid=(bsz, mh, nc),
        in_specs=[
            pl.BlockSpec((L, dh), lambda b, h, c: (b * nc + c, o_mq + h)),
            pl.BlockSpec((L, dh), lambda b, h, c: (b * nc + c, o_mq + mh + h)),
            pl.BlockSpec((L, dh), lambda b, h, c: (b * nc + c, o_mq + 2 * mh + h)),
            pl.BlockSpec((L, dh), lambda b, h, c: (b * nc + c, o_mq + 3 * mh + h)),
            pl.BlockSpec((L, LANES), lambda b, h, c: (b * nc + c, 0)),
            pl.BlockSpec((1, SUBLANES, L), lambda b, h, c: (b, 0, c)),
            pl.BlockSpec((conv, dh), lambda b, h, c: (0, h)),
            pl.BlockSpec((conv, dh), lambda b, h, c: (0, mh + h)),
            pl.BlockSpec((1, hb, dh), lambda b, h, c: (b, 0, h)),
            pl.BlockSpec((1, hb, dh), lambda b, h, c: (b, 0, mh + h)),
            pl.BlockSpec((1, dh, dh), lambda b, h, c: (b * mh + h, 0, 0)),
            pl.BlockSpec((1, 1, dh), lambda b, h, c: (b * mh + h, 0, 0)),
            pl.BlockSpec((1, 1, LANES), lambda b, h, c: (b * mh + h, 0, 0)),
            pl.BlockSpec((1, dh), lambda b, h, c: (0, h)),
        ],
        out_specs=[
            pl.BlockSpec((L, dh), lambda b, h, c: (b * nc + c, h)),
            pl.BlockSpec((1, dh, dh), lambda b, h, c: (b * mh + h, 0, 0)),
            pl.BlockSpec((1, 1, dh), lambda b, h, c: (b * mh + h, 0, 0)),
            pl.BlockSpec((1, 1, LANES), lambda b, h, c: (b * mh + h, 0, 0)),
        ],
        out_shape=[
            jax.ShapeDtypeStruct((r, cfg.mw), BF16),
            jax.ShapeDtypeStruct((bh, dh, dh), F32),
            jax.ShapeDtypeStruct((bh, 1, dh), F32),
            jax.ShapeDtypeStruct((bh, 1, LANES), F32),
        ],
        scratch_shapes=[pltpu.VMEM((L + SUBLANES, dh), F32), pltpu.VMEM((L + SUBLANES, dh), F32)],
        compiler_params=_cparams("parallel", "parallel", "arbitrary"),
        name="mlstm",
    )(z, z, z, z, g, gt, wqk, wqk, st_qk, st_qk, c0.reshape(bh, dh, dh), n0.reshape(bh, 1, dh), m0b,
      g_mout.reshape(1, cfg.mw))
    hm, c_new, n_new, m_new = outs
    return (hm, c_new.reshape(bsz, mh, dh, dh), n_new.reshape(bsz, mh, dh), m_new[:, 0, 0].reshape(bsz, mh))


def _gated_dual_kernel(ao_ref, hm_ref, ga_ref, gm_ref, wa_ref, wm_ref, o_ref):
    ya = jnp.dot(ao_ref[...], _weights_bf16(wa_ref), preferred_element_type=F32)
    ym = jnp.dot(hm_ref[...], _weights_bf16(wm_ref), preferred_element_type=F32)
    o_ref[...] = (_sigmoid(ga_ref[...]) * ya + _sigmoid(gm_ref[...]) * ym).astype(o_ref.dtype)


def _gated_dual(ao, hm, z, wa, wm, layer, gate_off):
    r = ao.shape[0]
    n = wa.shape[2]
    tm = _tile(r, 1024, 2 * SUBLANES)
    tn = _tile(n, 512, LANES)
    ga_b = gate_off // tn
    gm_b = (gate_off + n) // tn
    return pl.pallas_call(
        _gated_dual_kernel,
        grid=(r // tm, n // tn),
        in_specs=[
            pl.BlockSpec((tm, ao.shape[1]), lambda i, j: (i, 0)),
            pl.BlockSpec((tm, hm.shape[1]), lambda i, j: (i, 0)),
            pl.BlockSpec((tm, tn), lambda i, j: (i, ga_b + j)),
            pl.BlockSpec((tm, tn), lambda i, j: (i, gm_b + j)),
            _layer_block(layer, wa.shape[1], tn, lambda j: j),
            _layer_block(layer, wm.shape[1], tn, lambda j: j),
        ],
        out_specs=pl.BlockSpec((tm, tn), lambda i, j: (i, j)),
        out_shape=jax.ShapeDtypeStruct((r, n), BF16),
        compiler_params=_cparams("parallel", "arbitrary"),
        name="gated_dual",
    )(ao, hm, z, z, wa, wm)


def _mm_res_kernel(a_ref, w_ref, x_ref, o_ref):
    o_ref[...] = x_ref[...] + jnp.dot(a_ref[...], _weights_bf16(w_ref), preferred_element_type=F32)


def _mm_res(a, w, x, layer, *, tn_pref, name):
    r, k = a.shape
    n = w.shape[2]
    tm = _tile(r, 1024, 2 * SUBLANES)
    tn = _tile(n, tn_pref, LANES)
    return pl.pallas_call(
        _mm_res_kernel,
        grid=(r // tm, n // tn),
        in_specs=[
            pl.BlockSpec((tm, k), lambda i, j: (i, 0)),
            _layer_block(layer, k, tn, lambda j: j),
            pl.BlockSpec((tm, tn), lambda i, j: (i, j)),
        ],
        out_specs=pl.BlockSpec((tm, tn), lambda i, j: (i, j)),
        out_shape=jax.ShapeDtypeStruct((r, n), F32),
        compiler_params=_cparams("parallel", "arbitrary"),
        name=name,
    )(a, w, x)


def _ffn_up_kernel(x_ref, g_ref, wg_ref, wu_ref, st_ref, wc_ref, act_ref, stout_ref, hin_ref, xp_s, tail_s,
                   *, nseq, tp, conv, blocks_per_seq, t_end):
    i = pl.program_id(0)
    j = pl.program_id(1)
    hb = conv - 1
    top = SUBLANES

    @pl.when(j == 0)
    def _():
        hin_ref[...] = _rms(x_ref[...], g_ref[...]).astype(BF16)

    hin = hin_ref[...]
    gate = jnp.dot(hin, _weights_bf16(wg_ref), preferred_element_type=F32)
    up = jnp.dot(hin, _weights_bf16(wu_ref), preferred_element_type=F32)
    for s in range(nseq):
        rows = slice(s * tp, (s + 1) * tp)
        if blocks_per_seq > 1:
            prev = jnp.where((i % blocks_per_seq) == 0, st_ref[s], tail_s[j, top - hb:top])
        else:
            prev = st_ref[s]
        xp_s[top - hb:top] = prev
        xp_s[top:top + tp] = gate[rows]
        y = _causal_conv(xp_s, wc_ref, top, hb, tp)
        act_ref[rows] = (y * _sigmoid(y) * up[rows]).astype(act_ref.dtype)
        stout_ref[0, s] = xp_s[top + t_end - hb:top + t_end]
    if blocks_per_seq > 1:
        tail_s[j, top - hb:top] = xp_s[top + tp - hb:top + tp]


def _ffn_up(x, g, w, layer, st, wc, bsz, t, t_valid, cfg):
    r, k = x.shape
    dff, conv = cfg.dff, cfg.ffconv
    tn = _tile(dff, 512, LANES)
    ncb = dff // tn
    if t >= 256:
        assert t_valid == t
        tm = _tile(t, 1024, 2 * SUBLANES)
        nseq, tp, bps, t_end = 1, tm, t // tm, tm
    else:
        tm, nseq, tp, bps, t_end = r, bsz, t, 1, t_valid
    nrb = r // tm
    act, tails = pl.pallas_call(
        functools.partial(_ffn_up_kernel, nseq=nseq, tp=tp, conv=conv, blocks_per_seq=bps, t_end=t_end),
        grid=(nrb, ncb),
        in_specs=[
            pl.BlockSpec((tm, k), lambda i, j: (i, 0)),
            pl.BlockSpec((1, k), lambda i, j: (0, 0)),
            _layer_block(layer, k, tn, lambda j: j),
            _layer_block(layer, k, tn, lambda j: ncb + j),
            pl.BlockSpec((nseq, conv - 1, tn), lambda i, j: (i // bps if nseq == 1 else 0, 0, j)),
            pl.BlockSpec((conv, tn), lambda i, j: (0, j)),
        ],
        out_specs=[
            pl.BlockSpec((tm, tn), lambda i, j: (i, j)),
            pl.BlockSpec((1, nseq, conv - 1, tn), lambda i, j: (i, 0, 0, j)),
        ],
        out_shape=[jax.ShapeDtypeStruct((r, dff), BF16),
                   jax.ShapeDtypeStruct((nrb, nseq, conv - 1, dff), F32)],
        scratch_shapes=[pltpu.VMEM((tm, k), BF16), pltpu.VMEM((tp + SUBLANES, tn), F32),
                        pltpu.VMEM((ncb, SUBLANES, tn), F32)],
        compiler_params=_cparams("arbitrary", "arbitrary"),
        name="ffn_up",
    )(x, g, w, w, st, wc)
    if nseq == 1:
        return act, tails.reshape(bsz, bps, conv - 1, dff)[:, bps - 1]
    return act, tails[0]


def _layer(x, bsz, t, t_valid, layer, depth, lw, kv_all, st_qk, st_ff, c0, n0, m0, attn, want_vt, cfg):
    mw = cfg.mw
    lam_init = 0.8 - 0.6 * math.exp(-0.3 * layer)

    outs = _in_proj(x, lw['g_mix'], lw['w_in'], lw['n_a'], lw['w_g'], lw['g_qk'], kv_all, layer, depth, cfg,
                    t_seq=t if want_vt else None)
    z, side, k_all, v_all = outs[:4]
    vt = outs[4] if len(outs) > 4 else None
    g, gt = _gates(x, lw['g_mix'], lw['w_if'], lw['b_if'], bsz, t, cfg)
    ao = attn(side, vt, lw['w_lambda'], lw['g_sub'], lam_init)
    hm, c_new, n_new, m_new = _mlstm(z, g, gt, lw['w_qkconv'], st_qk, c0, n0, m0, lw['g_mout'], bsz, t, t_valid, cfg)
    merged = _gated_dual(ao, hm, z, lw['w_br_attn'], lw['w_br_mlstm'], layer, 4 * mw)
    x1 = _mm_res(merged, lw['w_out'], x, layer, tn_pref=512, name="out_proj")
    act, ff_new = _ffn_up(x1, lw['g_ffn'], lw['w_ff_in'], layer, st_ff, lw['w_ffconv'], bsz, t, t_valid, cfg)
    x2 = _mm_res(act, lw['w_ff_out'], x1, layer, tn_pref=256, name="ffn_down")

    qk_new = z.reshape(bsz, t, -1)[:, t_valid - (cfg.conv - 1):t_valid, :2 * mw]
    return x2, (k_all, v_all), (c_new, n_new, m_new, qk_new, ff_new)


def _layer_weights(l, cfg, g_mix, w_in, g_q, g_k, w_lambda, g_sub, w_qkconv, b_if, g_mout, w_br_attn,
                   w_br_mlstm, w_out, g_ffn, w_ff_in, w_ffconv, w_ff_out):
    qw, vw, mw, d = cfg.qw, cfg.vw, cfg.mw, cfg.d
    o_if = 2 * qw + vw + 4 * mw
    ng = 2 * cfg.mh
    w = w_in[l]
    reps = qw // cfg.e
    return {
        'g_mix': g_mix[l].reshape(1, d),
        'w_in': w_in,
        'n_a': o_if,
        'w_g': w[:, o_if + ng:].astype(BF16),
        'w_if': w[:, o_if:o_if + ng],
        'b_if': b_if[l].reshape(ng),
        'g_qk': jnp.concatenate([jnp.tile(g_q[l], reps), jnp.tile(g_k[l], reps)]).reshape(1, 2 * qw),
        'w_lambda': w_lambda[l],
        'g_sub': g_sub[l].reshape(1, cfg.dv),
        'w_qkconv': w_qkconv[l],
        'g_mout': g_mout[l],
        'w_br_attn': w_br_attn,
        'w_br_mlstm': w_br_mlstm,
        'w_out': w_out,
        'g_ffn': g_ffn[l].reshape(1, d),
        'w_ff_in': w_ff_in,
        'w_ffconv': w_ffconv[l],
        'w_ff_out': w_ff_out,
    }


def kernel(x_prompt, x_sample, cache_k, cache_v, page_table, state_C, state_n, state_m, state_qkconv,
           state_ffconv, g_mix, w_in, g_q, g_k, w_lambda, g_sub, w_qkconv, b_if, g_mout, w_br_attn,
           w_br_mlstm, w_out, g_ffn, w_ff_in, w_ffconv, w_ff_out):
    bp, tp_, d = x_prompt.shape
    bs, ts, _ = x_sample.shape
    depth = w_in.shape[0]
    cfg = Cfg(d=d, ah=cache_k.shape[3], e=cache_k.shape[4] // 2, dv=cache_v.shape[4], mh=state_C.shape[2],
              dh=state_C.shape[3], conv=w_qkconv.shape[1], dff=w_ff_out.shape[1], ffconv=w_ffconv.shape[1])
    assert ts <= SAMPLE_TPAD and ts >= max(cfg.conv, cfg.ffconv) - 1

    yp = x_prompt.reshape(bp * tp_, d)
    ys = jnp.zeros((bs, SAMPLE_TPAD, d), F32).at[:, :ts].set(x_sample).reshape(bs * SAMPLE_TPAD, d)
    zero_qk = jnp.zeros((bp, cfg.conv - 1, 2 * cfg.mw), F32)
    zero_ff = jnp.zeros((bp, cfg.ffconv - 1, cfg.dff), F32)
    zero_c = jnp.zeros((bp, cfg.mh, cfg.dh, cfg.dh), F32)
    zero_n = jnp.zeros((bp, cfg.mh, cfg.dh), F32)
    zero_m = jnp.zeros((bp, cfg.mh), F32)

    new_p = [[] for _ in range(5)]
    new_s = [[] for _ in range(5)]
    kv_p = kv_s = None
    for l in range(depth):
        lw = _layer_weights(l, cfg, g_mix, w_in, g_q, g_k, w_lambda, g_sub, w_qkconv, b_if, g_mout,
                            w_br_attn, w_br_mlstm, w_out, g_ffn, w_ff_in, w_ffconv, w_ff_out)
        p_attn = functools.partial(_prompt_attn, bsz=bp, t=tp_, cfg=cfg)
        yp, kv_p, sp = _layer(yp, bp, tp_, tp_, l, depth, lw, kv_p, zero_qk, zero_ff, zero_c, zero_n, zero_m,
                              lambda side, vt, wl, gs, li: p_attn(side, vt, wl, gs, lam_init=li), True, cfg)
        s_attn = functools.partial(_sample_attn, cache_k=cache_k, cache_v=cache_v, page_table=page_table,
                                   layer=l, ts=ts, cfg=cfg)
        ys, kv_s, ss = _layer(ys, bs, SAMPLE_TPAD, ts, l, depth, lw, kv_s, state_qkconv[l], state_ffconv[l],
                              state_C[l], state_n[l], state_m[l],
                              lambda side, vt, wl, gs, li: s_attn(side, wl=wl, gs=gs, lam_init=li), False, cfg)
        for j in range(5):
            new_p[j].append(sp[j])
            new_s[j].append(ss[j])
    outs_p = [jnp.stack(a, axis=0) for a in new_p]
    outs_s = [jnp.stack(a, axis=0) for a in new_s]
    kp = kv_p[0].reshape(depth, bp, tp_, cfg.ah, 2 * cfg.e)
    vp = kv_p[1].reshape(depth, bp, tp_, cfg.ah, cfg.dv)
    ks = kv_s[0].reshape(depth, bs, SAMPLE_TPAD, cfg.ah, 2 * cfg.e)[:, :, :ts]
    vs = kv_s[1].reshape(depth, bs, SAMPLE_TPAD, cfg.ah, cfg.dv)[:, :, :ts]
    y_prompt = yp.reshape(bp, tp_, d)
    y_sample = ys.reshape(bs, SAMPLE_TPAD, d)[:, :ts]
    return (y_prompt, y_sample, kp, vp, *outs_p, ks, vs, *outs_s)
```

```python
import functools
import math
from typing import NamedTuple

import numpy as np
import jax
import jax.numpy as jnp
from jax import lax
from jax.experimental import pallas as pl
from jax.experimental.pallas import tpu as pltpu

F32 = jnp.float32
BF16 = jnp.bfloat16
EPS = 1e-6
NEG_INF = float("-inf")
LOG2E = math.log2(math.e)

LANES = 128
SUBLANES = 8
SAMPLE_TPAD = 16
VMEM_LIMIT = 56 * 2 ** 20

NT_DIMS = (((1,), (1,)), ((), ()))
NN_DIMS = (((1,), (0,)), ((), ()))
TN_DIMS = (((0,), (0,)), ((), ()))


class Cfg(NamedTuple):
    d: int
    ah: int
    e: int
    dv: int
    mh: int
    dh: int
    conv: int
    dff: int
    ffconv: int

    @property
    def qw(self):
        return self.ah * 2 * self.e

    @property
    def vw(self):
        return self.ah * self.dv

    @property
    def mw(self):
        return self.mh * self.dh


def _tile(n, pref, mult):
    t = min(pref, n)
    t -= t % mult
    while t >= mult:
        if n % t == 0:
            return t
        t -= mult
    return n


def _cparams(*sem):
    return pltpu.CompilerParams(dimension_semantics=sem, vmem_limit_bytes=VMEM_LIMIT)


def _sigmoid(x):
    return 1.0 / (1.0 + jnp.exp(-x))


def _log_sigmoid(x):
    return -(jnp.maximum(-x, 0.0) + jnp.log1p(jnp.exp(-jnp.abs(x))))


def _rms(x, g):
    ms = jnp.mean(x * x, axis=-1, keepdims=True)
    return x * lax.rsqrt(ms + EPS) * g


def _split_bf16(a):
    hi = a.astype(BF16)
    lo = (a - hi.astype(F32)).astype(BF16)
    return hi, lo


def _dot3(a, b, dims):
    ah, al = _split_bf16(a)
    bh, bl = _split_bf16(b)
    f = lambda p, q: lax.dot_general(p, q, dims, preferred_element_type=F32)
    return f(ah, bh) + (f(ah, bl) + f(al, bh))


def _weights_bf16(w_ref):
    w = w_ref[0] if len(w_ref.shape) == 3 else w_ref[...]
    return w.astype(BF16)


def _layer_block(layer, k, tn, col_of):
    return pl.BlockSpec((1, k, tn), lambda i, j: (layer, 0, col_of(j)))


def _causal_conv(xp_s, w_ref, top, hb, rows):
    y = None
    for j in range(hb + 1):
        term = xp_s[top - hb + j:top - hb + j + rows] * w_ref[j:j + 1]
        y = term if y is None else y + term
    return y


def _in_proj_kernel(*refs, n_in, emit_vt, nqb, nvb, na, q_scale, tn):
    x_ref, g_ref, wa_ref, wgm_ref, wgn_ref, gn_ref = refs[:6]
    z_ref, side_ref, k_ref, v_ref = refs[n_in:n_in + 4]
    vt_ref = refs[n_in + 4] if emit_vt else None
    hin_ref = refs[-1]
    j = pl.program_id(1)
    jv0 = 2 * nqb
    j0 = jv0 + nvb

    @pl.when(j == 0)
    def _():
        hin_ref[...] = _rms(x_ref[...], g_ref[...]).astype(BF16)

    def proj(w_rows):
        return lax.dot_general(hin_ref[...], w_rows.astype(BF16), NT_DIMS, preferred_element_type=F32)

    def qk_norm(side_scale, f32_ref):
        acc = proj(wa_ref[0])
        for c in range(tn // LANES):
            sl = slice(c * LANES, (c + 1) * LANES)
            y = _rms(acc[:, sl], gn_ref[:, sl])
            if f32_ref is not None:
                f32_ref[:, sl] = y
            side_ref[:, sl] = (y if side_scale is None else y * side_scale).astype(BF16)

    @pl.when(j < nqb)
    def _():
        qk_norm(q_scale, None)

    @pl.when((j >= nqb) & (j < jv0))
    def _():
        qk_norm(None, k_ref)

    @pl.when((j >= jv0) & (j < j0))
    def _():
        acc = proj(wa_ref[0])
        v_ref[...] = acc
        side_ref[...] = acc.astype(BF16)
        if emit_vt:
            vt_ref[...] = acc.T.astype(BF16)

    @pl.when((j >= j0) & (j < na))
    def _():
        z_ref[...] = proj(wa_ref[0])

    @pl.when(j >= na)
    def _():
        z_ref[...] = proj(jnp.concatenate([wgm_ref[0, SUBLANES:, :], wgn_ref[0]], axis=0))


def _in_proj(x, g, w_t, n_a, n_g, gn, kv_all, layer, depth, cfg, t_seq=None):
    r, k = x.shape
    qw, vw = cfg.qw, cfg.vw
    n_side = 2 * qw + vw
    n_z = n_a - n_side + n_g
    tm = _tile(r, 1024, SUBLANES)
    tn = _tile(functools.reduce(math.gcd, (qw, vw, n_a, n_g)), 512, LANES)
    nqb, nvb, na, nsb = qw // tn, vw // tn, n_a // tn, n_side // tn
    nrb = r // tm
    assert w_t.shape[1] == n_a + SUBLANES + n_g and n_a % tn == 0
    tiles_per_block = tn // SUBLANES
    in_specs = [
        pl.BlockSpec((tm, k), lambda i, j: (i, 0)),
        pl.BlockSpec((1, k), lambda i, j: (0, 0)),
        pl.BlockSpec((1, tn, k), lambda i, j: (layer, jnp.minimum(j, na - 1), 0)),
        pl.BlockSpec((1, tn, k), lambda i, j: (layer, jnp.maximum(j, na), 0)),
        pl.BlockSpec((1, SUBLANES, k), lambda i, j: (layer, (jnp.maximum(j, na) + 1) * tiles_per_block, 0)),
        pl.BlockSpec((1, tn), lambda i, j: (0, jnp.minimum(j, 2 * nqb - 1))),
    ]
    args = [x, g, w_t, w_t, w_t, gn]
    aliases = {}
    if kv_all is not None:
        in_specs += [pl.BlockSpec(memory_space=pl.ANY), pl.BlockSpec(memory_space=pl.ANY)]
        args += list(kv_all)
        aliases = {6: 2, 7: 3}
    out_specs = [
        pl.BlockSpec((tm, tn), lambda i, j: (i, jnp.maximum(j - nsb, 0))),
        pl.BlockSpec((tm, tn), lambda i, j: (i, jnp.minimum(j, nsb - 1))),
        pl.BlockSpec((tm, tn), lambda i, j: (layer * nrb + i, jnp.clip(j - nqb, 0, nqb - 1))),
        pl.BlockSpec((tm, tn), lambda i, j: (layer * nrb + i, jnp.clip(j - 2 * nqb, 0, nvb - 1))),
    ]
    out_shape = [jax.ShapeDtypeStruct((r, n_z), F32), jax.ShapeDtypeStruct((r, n_side), BF16),
                 jax.ShapeDtypeStruct((depth * r, qw), F32), jax.ShapeDtypeStruct((depth * r, vw), F32)]
    emit_vt = t_seq is not None and t_seq % tm == 0
    if emit_vt:
        tpb = t_seq // tm
        out_specs.append(pl.BlockSpec(
            (tn, tm), lambda i, j: ((i // tpb) * nvb + jnp.clip(j - 2 * nqb, 0, nvb - 1), i % tpb)))
        out_shape.append(jax.ShapeDtypeStruct((r // t_seq * vw, t_seq), BF16))
    return pl.pallas_call(
        functools.partial(_in_proj_kernel, n_in=len(args), emit_vt=emit_vt, nqb=nqb, nvb=nvb, na=na,
                          q_scale=cfg.e ** -0.5 * LOG2E, tn=tn),
        grid=(nrb, na + n_g // tn),
        in_specs=in_specs,
        out_specs=out_specs,
        out_shape=out_shape,
        scratch_shapes=[pltpu.VMEM((tm, k), BF16)],
        input_output_aliases=aliases,
        compiler_params=_cparams("parallel", "arbitrary"),
        name="in_proj",
    )(*args)


def _gates_kernel(x_ref, g_ref, wt_ref, b_ref, bt_ref, o_ref, ot_ref, *, mh):
    hin = _rms(x_ref[...], g_ref[...])
    wt = wt_ref[0]
    wt_rows = jnp.concatenate([wt, jnp.zeros((LANES - SUBLANES, wt.shape[1]), F32)], axis=0)
    gif = _dot3(hin, wt_rows, NT_DIMS) + b_ref[...]
    lane = lax.broadcasted_iota(jnp.int32, gif.shape, 1)
    o_ref[...] = jnp.where(lane < mh, gif, _log_sigmoid(gif))
    gift = _dot3(wt, hin, NT_DIMS) + bt_ref[:, :1]
    sub = lax.broadcasted_iota(jnp.int32, gift.shape, 0)
    ot_ref[0] = jnp.where(sub < mh, gift, _log_sigmoid(gift))


def _gates(x, g, w_t, row0, layer, b_if, bsz, t, cfg):
    r, k = x.shape
    ng = 2 * cfg.mh
    assert ng == SUBLANES and row0 % SUBLANES == 0
    tm = _tile(t, 512, LANES) if t % LANES == 0 else t
    b = jnp.zeros((1, LANES), F32).at[0, :ng].set(b_if)
    bt = jnp.zeros((SUBLANES, LANES), F32).at[:ng].set(jnp.broadcast_to(b_if[:, None], (ng, LANES)))
    nt = t // tm
    return pl.pallas_call(
        functools.partial(_gates_kernel, mh=cfg.mh),
        grid=(bsz, nt),
        in_specs=[
            pl.BlockSpec((tm, k), lambda bi, ti: (bi * nt + ti, 0)),
            pl.BlockSpec((1, k), lambda bi, ti: (0, 0)),
            pl.BlockSpec((1, SUBLANES, k), lambda bi, ti: (layer, row0 // SUBLANES, 0)),
            pl.BlockSpec((1, LANES), lambda bi, ti: (0, 0)),
            pl.BlockSpec((SUBLANES, LANES), lambda bi, ti: (0, 0)),
        ],
        out_specs=[
            pl.BlockSpec((tm, LANES), lambda bi, ti: (bi * nt + ti, 0)),
            pl.BlockSpec((1, SUBLANES, tm), lambda bi, ti: (bi, 0, ti)),
        ],
        out_shape=[jax.ShapeDtypeStruct((r, LANES), F32),
                   jax.ShapeDtypeStruct((bsz, SUBLANES, t), F32)],
        compiler_params=_cparams("parallel", "parallel"),
        name="gates",
    )(x, g, w_t, b, bt)


def _lambda_vec(wl_ref, lam_init):
    wl = wl_ref[...]
    a = jnp.sum(wl[0:1] * wl[1:2], axis=-1, keepdims=True)
    b = jnp.sum(wl[2:3] * wl[3:4], axis=-1, keepdims=True)
    return jnp.exp(a) - jnp.exp(b) + lam_init


def _pattn_kernel(qi_ref, ki_ref, q_ref, k_ref, vt_ref, wl_ref, gs_ref, o_ref, m_s, l_s, acc_s,
                  *, tq, e, dv, hpb, lam_init):
    step = pl.program_id(2)
    qi = qi_ref[step]
    ki = ki_ref[step]

    @pl.when(ki == 0)
    def _():
        m_s[...] = jnp.full(m_s.shape, NEG_INF, F32)
        l_s[...] = jnp.zeros(l_s.shape, F32)
        acc_s[...] = jnp.zeros(acc_s.shape, F32)

    def body(masked):
        if masked:
            key = lax.broadcasted_iota(jnp.int32, (tq, tq), 0)
            qry = lax.broadcasted_iota(jnp.int32, (tq, tq), 1)
            keep = key <= qry
        for hc in range(2 * hpb):
            cols = slice(hc * e, (hc + 1) * e)
            st = lax.dot_general(k_ref[:, cols], q_ref[:, cols], NT_DIMS,
                                 preferred_element_type=F32)
            if masked:
                st = jnp.where(keep, st, NEG_INF)
            m_prev = m_s[hc]
            m_new = jnp.maximum(m_prev, jnp.max(st, axis=0, keepdims=True))
            alpha = jnp.exp2(m_prev - m_new)
            pt = jnp.exp2(st - m_new)
            l_s[hc] = alpha * l_s[hc] + jnp.sum(pt, axis=0, keepdims=True)
            m_s[hc] = m_new
            vt = vt_ref[(hc // 2) * dv:(hc // 2 + 1) * dv]
            acc_s[hc] = alpha * acc_s[hc] + jnp.dot(vt, pt.astype(BF16), preferred_element_type=F32)

    @pl.when(ki < qi)
    def _():
        body(False)

    @pl.when(ki == qi)
    def _():
        body(True)
        lam = _lambda_vec(wl_ref, lam_init)
        for h in range(hpb):
            ot = (acc_s[2 * h] * (1.0 / l_s[2 * h])
                  - lam * (acc_s[2 * h + 1] * (1.0 / l_s[2 * h + 1])))
            ms = jnp.mean(ot * ot, axis=0, keepdims=True)
            ot = ot * lax.rsqrt(ms + EPS) * gs_ref[...] * (1.0 - lam_init)
            o_ref[:, h * dv:(h + 1) * dv] = ot.T.astype(o_ref.dtype)


def _prompt_attn(side, vt, wl, gs, bsz, t, cfg, lam_init):
    r = side.shape[0]
    e, dv, ah = cfg.e, cfg.dv, cfg.ah
    tq = _tile(t, 512, LANES)
    nq = t // tq
    kb = cfg.qw // (2 * e)
    pairs = [(qi, ki) for qi in range(nq) for ki in range(qi + 1)]
    qi_tab = jnp.asarray(np.array([p[0] for p in pairs], np.int32))
    ki_tab = jnp.asarray(np.array([p[1] for p in pairs], np.int32))
    if vt is None:
        vt = side[:, 2 * cfg.qw:].reshape(bsz, t, ah, dv).transpose(0, 2, 3, 1).reshape(bsz * ah * dv, t)
    hpb = 4 if ah % 4 == 0 else (2 if ah % 2 == 0 else 1)
    nhb = ah // hpb
    kb = kb // hpb
    grid_spec = pltpu.PrefetchScalarGridSpec(
        num_scalar_prefetch=2,
        grid=(bsz, nhb, len(pairs)),
        in_specs=[
            pl.BlockSpec((tq, hpb * 2 * e), lambda b, h, s, qt, kt: (b * nq + qt[s], h)),
            pl.BlockSpec((tq, hpb * 2 * e), lambda b, h, s, qt, kt: (b * nq + kt[s], kb + h)),
            pl.BlockSpec((hpb * dv, tq), lambda b, h, s, qt, kt: (b * nhb + h, kt[s])),
            pl.BlockSpec((4, e), lambda b, h, s, qt, kt: (0, 0)),
            pl.BlockSpec((dv, 1), lambda b, h, s, qt, kt: (0, 0)),
        ],
        out_specs=pl.BlockSpec((tq, hpb * dv), lambda b, h, s, qt, kt: (b * nq + qt[s], h)),
        scratch_shapes=[pltpu.VMEM((2 * hpb, 1, tq), F32), pltpu.VMEM((2 * hpb, 1, tq), F32),
                        pltpu.VMEM((2 * hpb, dv, tq), F32)],
    )
    return pl.pallas_call(
        functools.partial(_pattn_kernel, tq=tq, e=e, dv=dv, hpb=hpb, lam_init=lam_init),
        grid_spec=grid_spec,
        out_shape=jax.ShapeDtypeStruct((r, cfg.vw), BF16),
        compiler_params=_cparams("parallel", "parallel", "arbitrary"),
        name="prompt_attn",
    )(qi_tab, ki_tab, side, side, vt, wl, gs.reshape(dv, 1))


def _sattn_kernel(pt_ref, q_ref, kn_ref, vn_ref, *refs, ah, e, dv, ts, tpad, nsteps, ppb, lam_init):
    del pt_ref
    kp_refs, vp_refs = refs[:ppb], refs[ppb:2 * ppb]
    wl_ref, gs_ref, o_ref, qs_s, bias_s, m_s, l_s, acc_s = refs[2 * ppb:]
    p = pl.program_id(1)
    rph = SUBLANES
    page = kp_refs[0].shape[2]
    row8 = lax.broadcasted_iota(jnp.int32, (rph, 2 * e), 0)
    lane = lax.broadcasted_iota(jnp.int32, (rph, 2 * e), 1)

    m_rows = ah * rph

    def row_head(shape):
        return lax.shift_right_logical(lax.broadcasted_iota(jnp.int32, shape, 0), int(math.log2(rph)))

    @pl.when(p == 0)
    def _():
        for h in range(ah):
            qh = q_ref[0:rph, h * 2 * e:(h + 1) * 2 * e].astype(F32)
            first = jnp.where((row8 < ts) & (lane < e), qh, 0.0)
            second = jnp.where((row8 >= ts) & (row8 < 2 * ts) & (lane >= e), pltpu.roll(qh, ts, 0), 0.0)
            qs_s[h * rph:(h + 1) * rph] = first + second
        c_head = lax.broadcasted_iota(jnp.int32, bias_s.shape, 1) & (ah - 1)
        bias_s[...] = jnp.where(row_head(bias_s.shape) == c_head, 0.0, NEG_INF)
        m_s[...] = jnp.full(m_s.shape, NEG_INF, F32)
        l_s[...] = jnp.zeros(l_s.shape, F32)
        acc_s[...] = jnp.zeros(acc_s.shape, F32)

    def online_update(scores, pv):
        m_prev = m_s[...]
        m_new = m_prev
        for s in scores:
            m_new = jnp.maximum(m_new, jnp.max(s, axis=-1, keepdims=True))
        alpha = jnp.exp2(m_prev - m_new)
        l_new = alpha * l_s[...]
        acc = alpha * acc_s[...]
        for i, s in enumerate(scores):
            pr = jnp.exp2(s - m_new)
            l_new = l_new + jnp.sum(pr, axis=-1, keepdims=True)
            acc = acc + pv(i, pr)
        l_s[...] = l_new
        m_s[...] = m_new
        acc_s[...] = acc

    qs = qs_s[...].astype(BF16)
    bias = bias_s[...]
    scores = [lax.dot_general(qs, kp_refs[i][0, 0].reshape(page * ah, 2 * e).astype(BF16), NT_DIMS,
                              preferred_element_type=F32) + bias for i in range(ppb)]
    online_update(scores, lambda i, pr: jnp.dot(pr.astype(BF16), vp_refs[i][0, 0].reshape(page * ah, dv).astype(BF16),
                                                preferred_element_type=F32))

    @pl.when(p == nsteps - 1)
    def _():
        rh = row_head((m_rows, tpad))
        row = lax.broadcasted_iota(jnp.int32, (m_rows, tpad), 0) & (rph - 1)
        qt = jnp.where(row < ts, row, row - ts)
        kt = lax.broadcasted_iota(jnp.int32, (m_rows, tpad), 1)
        s_new = jnp.full((m_rows, tpad), NEG_INF, F32)
        for h in range(ah):
            kh = kn_ref[:, (ah + h) * 2 * e:(ah + h + 1) * 2 * e]
            sh = lax.dot_general(qs, kh, NT_DIMS, preferred_element_type=F32)
            s_new = jnp.where((rh == h) & (kt <= qt), sh, s_new)

        def pv_new(_, pr):
            out = jnp.zeros((m_rows, dv), F32)
            for h in range(ah):
                ph = jnp.where(rh == h, pr, 0.0).astype(BF16)
                out = out + jnp.dot(ph, vn_ref[:, h * dv:(h + 1) * dv], preferred_element_type=F32)
            return out

        online_update([s_new], pv_new)

        lam = _lambda_vec(wl_ref, lam_init)
        zeros = jnp.zeros((tpad - rph, dv), F32)
        for h in range(ah):
            rows = slice(h * rph, (h + 1) * rph)
            a = acc_s[rows] * (1.0 / l_s[rows])
            o = a - lam * pltpu.roll(a, rph - ts, 0)
            o = _rms(o, gs_ref[...]) * (1.0 - lam_init)
            o = jnp.where(lax.broadcasted_iota(jnp.int32, o.shape, 0) < ts, o, 0.0)
            o_ref[:, h * dv:(h + 1) * dv] = jnp.concatenate([o, zeros], axis=0).astype(o_ref.dtype)


def _sample_attn(side, cache_k, cache_v, page_table, layer, wl, gs, ts, cfg, lam_init):
    r = side.shape[0]
    e, dv, ah = cfg.e, cfg.dv, cfg.ah
    tpad = SAMPLE_TPAD
    bsz, npages = page_table.shape
    page = cache_k.shape[2]
    assert 2 * ts <= SUBLANES and ah & (ah - 1) == 0
    m_rows = ah * SUBLANES
    ppb = _tile(npages, 4, 1)

    def page_spec(width, i):
        return pl.BlockSpec((1, 1, page, ah, width),
                            lambda b, p, pt: (layer, pt[b * npages + p * ppb + i], 0, 0, 0))

    grid_spec = pltpu.PrefetchScalarGridSpec(
        num_scalar_prefetch=1,
        grid=(bsz, npages // ppb),
        in_specs=[
            pl.BlockSpec((tpad, cfg.qw), lambda b, p, pt: (b, 0)),
            pl.BlockSpec((tpad, 2 * cfg.qw), lambda b, p, pt: (b, 0)),
            pl.BlockSpec((tpad, cfg.vw), lambda b, p, pt: (b, 2 * cfg.qw // cfg.vw)),
            *[page_spec(2 * e, i) for i in range(ppb)],
            *[page_spec(dv, i) for i in range(ppb)],
            pl.BlockSpec((4, e), lambda b, p, pt: (0, 0)),
            pl.BlockSpec((1, dv), lambda b, p, pt: (0, 0)),
        ],
        out_specs=pl.BlockSpec((tpad, cfg.vw), lambda b, p, pt: (b, 0)),
        scratch_shapes=[pltpu.VMEM((m_rows, 2 * e), F32), pltpu.VMEM((m_rows, page * ah), F32),
                        pltpu.VMEM((m_rows, 1), F32), pltpu.VMEM((m_rows, 1), F32), pltpu.VMEM((m_rows, dv), F32)],
    )
    return pl.pallas_call(
        functools.partial(_sattn_kernel, ah=ah, e=e, dv=dv, ts=ts, tpad=tpad, nsteps=npages // ppb, ppb=ppb,
                          lam_init=lam_init),
        grid_spec=grid_spec,
        out_shape=jax.ShapeDtypeStruct((r, cfg.vw), BF16),
        compiler_params=_cparams("parallel", "arbitrary"),
        name="sample_attn",
    )(page_table.reshape(-1), side, side, side, *([cache_k] * ppb), *([cache_v] * ppb), wl, gs)


def _mlstm_kernel(mq_ref, mk_ref, mv_ref, mo_ref, g_ref, gt_ref, wq_ref, wk_ref, stq_ref, stk_ref,
                  c0_ref, n0_ref, m0_ref, gm_ref, hm_ref, c_ref, n_ref, m_ref, xq_s, xk_s,
                  *, L, dh, mh, conv, t_valid, t_total):
    h = pl.program_id(1)
    c = pl.program_id(2)
    hb = conv - 1
    top = SUBLANES

    @pl.when(c == 0)
    def _():
        xq_s[top - hb:top] = stq_ref[0]
        xk_s[top - hb:top] = stk_ref[0]
        c_ref[0] = c0_ref[0]
        n_ref[0] = n0_ref[0]
        m_ref[0] = m0_ref[0]

    xq_s[top:top + L] = mq_ref[...]
    xk_s[top:top + L] = mk_ref[...]
    yq = _causal_conv(xq_s, wq_ref, top, hb, L)
    yk = _causal_conv(xk_s, wk_ref, top, hb, L)
    q = yq * _sigmoid(yq)
    k = yk * _sigmoid(yk) * dh ** -0.5
    xq_s[top - hb:top] = xq_s[top + L - hb:top + L]
    xk_s[top - hb:top] = xk_s[top + L - hb:top + L]

    g = g_ref[...]
    lane = lax.broadcasted_iota(jnp.int32, g.shape, 1)
    ic_col = jnp.sum(jnp.where(lane == h, g, 0.0), axis=-1, keepdims=True)
    fc_col = jnp.sum(jnp.where(lane == mh + h, g, 0.0), axis=-1, keepdims=True)
    gt = gt_ref[0]
    sub = lax.broadcasted_iota(jnp.int32, gt.shape, 0)
    ic_row = jnp.sum(jnp.where(sub == h, gt, 0.0), axis=0, keepdims=True)
    fc_row = jnp.sum(jnp.where(sub == mh + h, gt, 0.0), axis=0, keepdims=True)
    if t_valid < t_total:
        t_col = c * L + lax.broadcasted_iota(jnp.int32, (L, 1), 0)
        t_row = c * L + lax.broadcasted_iota(jnp.int32, (1, L), 1)
        ic_col = jnp.where(t_col < t_valid, ic_col, NEG_INF)
        fc_col = jnp.where(t_col < t_valid, fc_col, 0.0)
        ic_row = jnp.where(t_row < t_valid, ic_row, NEG_INF)
        fc_row = jnp.where(t_row < t_valid, fc_row, 0.0)

    row = lax.broadcasted_iota(jnp.int32, (L, L), 0)
    col = lax.broadcasted_iota(jnp.int32, (L, L), 1)
    causal = col <= row
    bcum_col = jnp.sum(jnp.where(causal, fc_row, 0.0), axis=-1, keepdims=True)
    bcum_row = jnp.sum(jnp.where(row <= col, fc_col, 0.0), axis=0, keepdims=True)

    m_prev = m_ref[0][:, :1]
    n_prev = n_ref[0]
    c_prev = c_ref[0]

    dmat = jnp.where(causal, bcum_col - bcum_row + ic_row, NEG_INF)
    inter = bcum_col + m_prev
    m_t = jnp.maximum(inter, jnp.max(dmat, axis=-1, keepdims=True))
    w_intra = jnp.exp(dmat - m_t)
    w_inter = jnp.exp(inter - m_t)

    qb = q.astype(BF16)
    kb = k.astype(BF16)
    vb = mv_ref[...].astype(BF16)
    s = lax.dot_general(qb, kb, NT_DIMS, preferred_element_type=F32) * w_intra
    num = (w_inter * jnp.dot(qb, c_prev.astype(BF16), preferred_element_type=F32)
           + jnp.dot(s.astype(BF16), vb, preferred_element_type=F32))
    den = w_inter * jnp.sum(q * n_prev, axis=-1, keepdims=True) + jnp.sum(s, axis=-1, keepdims=True)
    hc = num * (1.0 / jnp.maximum(jnp.abs(den), jnp.exp(-m_t)))

    m_new = m_t[L - 1:L]
    b_last = bcum_col[L - 1:L]
    w_s = jnp.exp(b_last - bcum_col + ic_col - m_new)
    decay = jnp.exp(b_last + m_prev - m_new)
    kw = k * w_s
    c_ref[0] = decay * c_prev + lax.dot_general(kw.astype(BF16), vb, TN_DIMS, preferred_element_type=F32)
    n_ref[0] = decay * n_prev + jnp.sum(kw, axis=0, keepdims=True)
    m_ref[0] = jnp.broadcast_to(m_new, m_ref.shape[1:])

    hm_ref[...] = (_rms(hc, gm_ref[...]) * _sigmoid(mo_ref[...])).astype(hm_ref.dtype)


def _mlstm(z, g, gt, wqk, st_qk, c0, n0, m0, g_mout, bsz, t, t_valid, cfg):
    r = z.shape[0]
    dh, mh, conv = cfg.dh, cfg.mh, cfg.conv
    L = _tile(t, 256, LANES) if t % LANES == 0 else t
    nc = t // L
    o_mq = 0
    bh = bsz * mh
    hb = conv - 1
    m0b = jnp.broadcast_to(m0.reshape(bh, 1, 1), (bh, 1, LANES))
    outs = pl.pallas_call(
        functools.partial(_mlstm_kernel, L=L, dh=dh, mh=mh, conv=conv, t_valid=t_valid, t_total=t),
        grid=(bsz, mh, nc),
        in_specs=[
            pl.BlockSpec((L, dh), lambda b, h, c: (b * nc + c, o_mq + h)),
            pl.BlockSpec((L, dh), lambda b, h, c: (b * nc + c, o_mq + mh + h)),
            pl.BlockSpec((L, dh), lambda b, h, c: (b * nc + c, o_mq + 2 * mh + h)),
            pl.BlockSpec((L, dh), lambda b, h, c: (b * nc + c, o_mq + 3 * mh + h)),
            pl.BlockSpec((L, LANES), lambda b, h, c: (b * nc + c, 0)),
            pl.BlockSpec((1, SUBLANES, L), lambda b, h, c: (b, 0, c)),
            pl.BlockSpec((conv, dh), lambda b, h, c: (0, h)),
            pl.BlockSpec((conv, dh), lambda b, h, c: (0, mh + h)),
            pl.BlockSpec((1, hb, dh), lambda b, h, c: (b, 0, h)),
            pl.BlockSpec((1, hb, dh), lambda b, h, c: (b, 0, mh + h)),
            pl.BlockSpec((1, dh, dh), lambda b, h, c: (b * mh + h, 0, 0)),
            pl.BlockSpec((1, 1, dh), lambda b, h, c: (b * mh + h, 0, 0)),
            pl.BlockSpec((1, 1, LANES), lambda b, h, c: (b * mh + h, 0, 0)),
            pl.BlockSpec((1, dh), lambda b, h, c: (0, h)),
        ],
        out_specs=[
            pl.BlockSpec((L, dh), lambda b, h, c: (b * nc + c, h)),
            pl.BlockSpec((1, dh, dh), lambda b, h, c: (b * mh + h, 0, 0)),
            pl.BlockSpec((1, 1, dh), lambda b, h, c: (b * mh + h, 0, 0)),
            pl.BlockSpec((1, 1, LANES), lambda b, h, c: (b * mh + h, 0, 0)),
        ],
        out_shape=[
            jax.ShapeDtypeStruct((r, cfg.mw), BF16),
            jax.ShapeDtypeStruct((bh, dh, dh), F32),
            jax.ShapeDtypeStruct((bh, 1, dh), F32),
            jax.ShapeDtypeStruct((bh, 1, LANES), F32),
        ],
        scratch_shapes=[pltpu.VMEM((L + SUBLANES, dh), F32), pltpu.VMEM((L + SUBLANES, dh), F32)],
        compiler_params=_cparams("parallel", "parallel", "arbitrary"),
        name="mlstm",
    )(z, z, z, z, g, gt, wqk, wqk, st_qk, st_qk, c0.reshape(bh, dh, dh), n0.reshape(bh, 1, dh), m0b,
      g_mout.reshape(1, cfg.mw))
    hm, c_new, n_new, m_new = outs
    return (hm, c_new.reshape(bsz, mh, dh, dh), n_new.reshape(bsz, mh, dh), m_new[:, 0, 0].reshape(bsz, mh))


def _gated_dual_kernel(ao_ref, hm_ref, ga_ref, gm_ref, wa_ref, wm_ref, o_ref):
    ya = jnp.dot(ao_ref[...], _weights_bf16(wa_ref), preferred_element_type=F32)
    ym = jnp.dot(hm_ref[...], _weights_bf16(wm_ref), preferred_element_type=F32)
    o_ref[...] = (_sigmoid(ga_ref[...]) * ya + _sigmoid(gm_ref[...]) * ym).astype(o_ref.dtype)


def _gated_dual(ao, hm, z, wa, wm, layer, gate_off):
    r = ao.shape[0]
    n = wa.shape[2]
    tm = _tile(r, 1024, 2 * SUBLANES)
    tn = _tile(n, 512, LANES)
    ga_b = gate_off // tn
    gm_b = (gate_off + n) // tn
    return pl.pallas_call(
        _gated_dual_kernel,
        grid=(r // tm, n // tn),
        in_specs=[
            pl.BlockSpec((tm, ao.shape[1]), lambda i, j: (i, 0)),
            pl.BlockSpec((tm, hm.shape[1]), lambda i, j: (i, 0)),
            pl.BlockSpec((tm, tn), lambda i, j: (i, ga_b + j)),
            pl.BlockSpec((tm, tn), lambda i, j: (i, gm_b + j)),
            _layer_block(layer, wa.shape[1], tn, lambda j: j),
            _layer_block(layer, wm.shape[1], tn, lambda j: j),
        ],
        out_specs=pl.BlockSpec((tm, tn), lambda i, j: (i, j)),
        out_shape=jax.ShapeDtypeStruct((r, n), BF16),
        compiler_params=_cparams("parallel", "arbitrary"),
        name="gated_dual",
    )(ao, hm, z, z, wa, wm)


def _mm_res_kernel(a_ref, w_ref, x_ref, o_ref):
    o_ref[...] = x_ref[...] + jnp.dot(a_ref[...], _weights_bf16(w_ref), preferred_element_type=F32)


def _mm_res(a, w, x, layer, *, tn_pref, name):
    r, k = a.shape
    n = w.shape[2]
    tm = _tile(r, 1024, 2 * SUBLANES)
    tn = _tile(n, tn_pref, LANES)
    return pl.pallas_call(
        _mm_res_kernel,
        grid=(r // tm, n // tn),
        in_specs=[
            pl.BlockSpec((tm, k), lambda i, j: (i, 0)),
            _layer_block(layer, k, tn, lambda j: j),
            pl.BlockSpec((tm, tn), lambda i, j: (i, j)),
        ],
        out_specs=pl.BlockSpec((tm, tn), lambda i, j: (i, j)),
        out_shape=jax.ShapeDtypeStruct((r, n), F32),
        compiler_params=_cparams("parallel", "arbitrary"),
        name=name,
    )(a, w, x)


def _ffn_up_kernel(x_ref, g_ref, wg_ref, wu_ref, st_ref, wc_ref, act_ref, stout_ref, hin_ref, xp_s, tail_s,
                   *, nseq, tp, conv, blocks_per_seq, t_end):
    i = pl.program_id(0)
    j = pl.program_id(1)
    hb = conv - 1
    top = SUBLANES

    @pl.when(j == 0)
    def _():
        hin_ref[...] = _rms(x_ref[...], g_ref[...]).astype(BF16)

    hin = hin_ref[...]
    gate = jnp.dot(hin, _weights_bf16(wg_ref), preferred_element_type=F32)
    up = jnp.dot(hin, _weights_bf16(wu_ref), preferred_element_type=F32)
    for s in range(nseq):
        rows = slice(s * tp, (s + 1) * tp)
        if blocks_per_seq > 1:
            prev = jnp.where((i % blocks_per_seq) == 0, st_ref[s], tail_s[j, top - hb:top])
        else:
            prev = st_ref[s]
        xp_s[top - hb:top] = prev
        xp_s[top:top + tp] = gate[rows]
        y = _causal_conv(xp_s, wc_ref, top, hb, tp)
        act_ref[rows] = (y * _sigmoid(y) * up[rows]).astype(act_ref.dtype)
        stout_ref[0, s] = xp_s[top + t_end - hb:top + t_end]
    if blocks_per_seq > 1:
        tail_s[j, top - hb:top] = xp_s[top + tp - hb:top + tp]


def _ffn_up(x, g, w, layer, st, wc, bsz, t, t_valid, cfg):
    r, k = x.shape
    dff, conv = cfg.dff, cfg.ffconv
    tn = _tile(dff, 512, LANES)
    ncb = dff // tn
    if t >= 256:
        assert t_valid == t
        tm = _tile(t, 1024, 2 * SUBLANES)
        nseq, tp, bps, t_end = 1, tm, t // tm, tm
    else:
        tm, nseq, tp, bps, t_end = r, bsz, t, 1, t_valid
    nrb = r // tm
    act, tails = pl.pallas_call(
        functools.partial(_ffn_up_kernel, nseq=nseq, tp=tp, conv=conv, blocks_per_seq=bps, t_end=t_end),
        grid=(nrb, ncb),
        in_specs=[
            pl.BlockSpec((tm, k), lambda i, j: (i, 0)),
            pl.BlockSpec((1, k), lambda i, j: (0, 0)),
            _layer_block(layer, k, tn, lambda j: j),
            _layer_block(layer, k, tn, lambda j: ncb + j),
            pl.BlockSpec((nseq, conv - 1, tn), lambda i, j: (i // bps if nseq == 1 else 0, 0, j)),
            pl.BlockSpec((conv, tn), lambda i, j: (0, j)),
        ],
        out_specs=[
            pl.BlockSpec((tm, tn), lambda i, j: (i, j)),
            pl.BlockSpec((1, nseq, conv - 1, tn), lambda i, j: (i, 0, 0, j)),
        ],
        out_shape=[jax.ShapeDtypeStruct((r, dff), BF16),
                   jax.ShapeDtypeStruct((nrb, nseq, conv - 1, dff), F32)],
        scratch_shapes=[pltpu.VMEM((tm, k), BF16), pltpu.VMEM((tp + SUBLANES, tn), F32),
                        pltpu.VMEM((ncb, SUBLANES, tn), F32)],
        compiler_params=_cparams("arbitrary", "arbitrary"),
        name="ffn_up",
    )(x, g, w, w, st, wc)
    if nseq == 1:
        return act, tails.reshape(bsz, bps, conv - 1, dff)[:, bps - 1]
    return act, tails[0]


def _layer(x, bsz, t, t_valid, layer, depth, lw, kv_all, st_qk, st_ff, c0, n0, m0, attn, want_vt, cfg):
    mw = cfg.mw
    lam_init = 0.8 - 0.6 * math.exp(-0.3 * layer)

    outs = _in_proj(x, lw['g_mix'], lw['w_t'], lw['n_a'], 2 * cfg.d, lw['g_qk'], kv_all, layer, depth, cfg,
                    t_seq=t if want_vt else None)
    z, side, k_all, v_all = outs[:4]
    vt = outs[4] if len(outs) > 4 else None
    g, gt = _gates(x, lw['g_mix'], lw['w_t'], lw['n_a'], layer, lw['b_if'], bsz, t, cfg)
    ao = attn(side, vt, lw['w_lambda'], lw['g_sub'], lam_init)
    hm, c_new, n_new, m_new = _mlstm(z, g, gt, lw['w_qkconv'], st_qk, c0, n0, m0, lw['g_mout'], bsz, t, t_valid, cfg)
    merged = _gated_dual(ao, hm, z, lw['w_br_attn'], lw['w_br_mlstm'], layer, 4 * mw)
    x1 = _mm_res(merged, lw['w_out'], x, layer, tn_pref=512, name="out_proj")
    act, ff_new = _ffn_up(x1, lw['g_ffn'], lw['w_ff_in'], layer, st_ff, lw['w_ffconv'], bsz, t, t_valid, cfg)
    x2 = _mm_res(act, lw['w_ff_out'], x1, layer, tn_pref=256, name="ffn_down")

    qk_new = z.reshape(bsz, t, -1)[:, t_valid - (cfg.conv - 1):t_valid, :2 * mw]
    return x2, (k_all, v_all), (c_new, n_new, m_new, qk_new, ff_new)


def _layer_weights(l, cfg, w_t, g_mix, g_q, g_k, w_lambda, g_sub, w_qkconv, b_if, g_mout, w_br_attn,
                   w_br_mlstm, w_out, g_ffn, w_ff_in, w_ffconv, w_ff_out):
    qw, vw, mw, d = cfg.qw, cfg.vw, cfg.mw, cfg.d
    ng = 2 * cfg.mh
    reps = qw // cfg.e
    return {
        'g_mix': g_mix[l].reshape(1, d),
        'w_t': w_t,
        'n_a': 2 * qw + vw + 4 * mw,
        'b_if': b_if[l].reshape(ng),
        'g_qk': jnp.concatenate([jnp.tile(g_q[l], reps), jnp.tile(g_k[l], reps)]).reshape(1, 2 * qw),
        'w_lambda': w_lambda[l],
        'g_sub': g_sub[l].reshape(1, cfg.dv),
        'w_qkconv': w_qkconv[l],
        'g_mout': g_mout[l],
        'w_br_attn': w_br_attn,
        'w_br_mlstm': w_br_mlstm,
        'w_out': w_out,
        'g_ffn': g_ffn[l].reshape(1, d),
        'w_ff_in': w_ff_in,
        'w_ffconv': w_ffconv[l],
        'w_ff_out': w_ff_out,
    }


def kernel(x_prompt, x_sample, cache_k, cache_v, page_table, state_C, state_n, state_m, state_qkconv,
           state_ffconv, g_mix, w_in, g_q, g_k, w_lambda, g_sub, w_qkconv, b_if, g_mout, w_br_attn,
           w_br_mlstm, w_out, g_ffn, w_ff_in, w_ffconv, w_ff_out):
    bp, tp_, d = x_prompt.shape
    bs, ts, _ = x_sample.shape
    depth = w_in.shape[0]
    cfg = Cfg(d=d, ah=cache_k.shape[3], e=cache_k.shape[4] // 2, dv=cache_v.shape[4], mh=state_C.shape[2],
              dh=state_C.shape[3], conv=w_qkconv.shape[1], dff=w_ff_out.shape[1], ffconv=w_ffconv.shape[1])
    assert ts <= SAMPLE_TPAD and ts >= max(cfg.conv, cfg.ffconv) - 1

    yp = x_prompt.reshape(bp * tp_, d)
    ys = jnp.zeros((bs, SAMPLE_TPAD, d), F32).at[:, :ts].set(x_sample).reshape(bs * SAMPLE_TPAD, d)
    zero_qk = jnp.zeros((bp, cfg.conv - 1, 2 * cfg.mw), F32)
    zero_ff = jnp.zeros((bp, cfg.ffconv - 1, cfg.dff), F32)
    zero_c = jnp.zeros((bp, cfg.mh, cfg.dh, cfg.dh), F32)
    zero_n = jnp.zeros((bp, cfg.mh, cfg.dh), F32)
    zero_m = jnp.zeros((bp, cfg.mh), F32)

    new_p = [[] for _ in range(5)]
    new_s = [[] for _ in range(5)]
    kv_p = kv_s = None
    w_t = jnp.swapaxes(w_in, 1, 2)
    for l in range(depth):
        lw = _layer_weights(l, cfg, w_t, g_mix, g_q, g_k, w_lambda, g_sub, w_qkconv, b_if, g_mout,
                            w_br_attn, w_br_mlstm, w_out, g_ffn, w_ff_in, w_ffconv, w_ff_out)
        p_attn = functools.partial(_prompt_attn, bsz=bp, t=tp_, cfg=cfg)
        yp, kv_p, sp = _layer(yp, bp, tp_, tp_, l, depth, lw, kv_p, zero_qk, zero_ff, zero_c, zero_n, zero_m,
                              lambda side, vt, wl, gs, li: p_attn(side, vt, wl, gs, lam_init=li), True, cfg)
        s_attn = functools.partial(_sample_attn, cache_k=cache_k, cache_v=cache_v, page_table=page_table,
                                   layer=l, ts=ts, cfg=cfg)
        ys, kv_s, ss = _layer(ys, bs, SAMPLE_TPAD, ts, l, depth, lw, kv_s, state_qkconv[l], state_ffconv[l],
                              state_C[l], state_n[l], state_m[l],
                              lambda side, vt, wl, gs, li: s_attn(side, wl=wl, gs=gs, lam_init=li), False, cfg)
        for j in range(5):
            new_p[j].append(sp[j])
            new_s[j].append(ss[j])
    outs_p = [jnp.stack(a, axis=0) for a in new_p]
    outs_s = [jnp.stack(a, axis=0) for a in new_s]
    kp = kv_p[0].reshape(depth, bp, tp_, cfg.ah, 2 * cfg.e)
    vp = kv_p[1].reshape(depth, bp, tp_, cfg.ah, cfg.dv)
    ks = kv_s[0].reshape(depth, bs, SAMPLE_TPAD, cfg.ah, 2 * cfg.e)[:, :, :ts]
    vs = kv_s[1].reshape(depth, bs, SAMPLE_TPAD, cfg.ah, cfg.dv)[:, :, :ts]
    y_prompt = yp.reshape(bp, tp_, d)
    y_sample = ys.reshape(bs, SAMPLE_TPAD, d)[:, :ts]
    return (y_prompt, y_sample, kp, vp, *outs_p, ks, vs, *outs_s)
```

```python
import functools
import math
from typing import NamedTuple

import numpy as np
import jax
import jax.numpy as jnp
from jax import lax
from jax.experimental import pallas as pl
from jax.experimental.pallas import tpu as pltpu

F32 = jnp.float32
BF16 = jnp.bfloat16
EPS = 1e-6
NEG_INF = float("-inf")
LOG2E = math.log2(math.e)

LANES = 128
SUBLANES = 8
SAMPLE_TPAD = 16
VMEM_LIMIT = 56 * 2 ** 20

NT_DIMS = (((1,), (1,)), ((), ()))
NN_DIMS = (((1,), (0,)), ((), ()))
TN_DIMS = (((0,), (0,)), ((), ()))


class Cfg(NamedTuple):
    d: int
    ah: int
    e: int
    dv: int
    mh: int
    dh: int
    conv: int
    dff: int
    ffconv: int

    @property
    def qw(self):
        return self.ah * 2 * self.e

    @property
    def vw(self):
        return self.ah * self.dv

    @property
    def mw(self):
        return self.mh * self.dh


def _tile(n, pref, mult):
    t = min(pref, n)
    t -= t % mult
    while t >= mult:
        if n % t == 0:
            return t
        t -= mult
    return n


def _cparams(*sem):
    return pltpu.CompilerParams(dimension_semantics=sem, vmem_limit_bytes=VMEM_LIMIT)


def _sigmoid(x):
    return 1.0 / (1.0 + jnp.exp(-x))


def _log_sigmoid(x):
    return -(jnp.maximum(-x, 0.0) + jnp.log1p(jnp.exp(-jnp.abs(x))))


def _rms(x, g):
    ms = jnp.mean(x * x, axis=-1, keepdims=True)
    return x * lax.rsqrt(ms + EPS) * g


def _split_bf16(a):
    hi = a.astype(BF16)
    lo = (a - hi.astype(F32)).astype(BF16)
    return hi, lo


def _dot3(a, b, dims):
    ah, al = _split_bf16(a)
    bh, bl = _split_bf16(b)
    f = lambda p, q: lax.dot_general(p, q, dims, preferred_element_type=F32)
    return f(ah, bh) + (f(ah, bl) + f(al, bh))


def _weights_bf16(w_ref):
    w = w_ref[0] if len(w_ref.shape) == 3 else w_ref[...]
    return w.astype(BF16)


def _layer_block(layer, k, tn, col_of):
    return pl.BlockSpec((1, k, tn), lambda i, j: (layer, 0, col_of(j)))


def _causal_conv(xp_s, w_ref, top, hb, rows):
    y = None
    for j in range(hb + 1):
        term = xp_s[top - hb + j:top - hb + j + rows] * w_ref[j:j + 1]
        y = term if y is None else y + term
    return y


def _in_proj_kernel(*refs, n_in, emit_vt, nqb, nvb, na, q_scale, tn):
    x_ref, g_ref, wa_ref, wgm_ref, wgn_ref, gn_ref = refs[:6]
    z_ref, side_ref, k_ref, v_ref = refs[n_in:n_in + 4]
    vt_ref = refs[n_in + 4] if emit_vt else None
    hin_ref = refs[-1]
    j = pl.program_id(1)
    jv0 = 2 * nqb
    j0 = jv0 + nvb

    @pl.when(j == 0)
    def _():
        hin_ref[...] = _rms(x_ref[...], g_ref[...]).astype(BF16)

    def proj(w_rows):
        return lax.dot_general(hin_ref[...], w_rows.astype(BF16), NT_DIMS, preferred_element_type=F32)

    def qk_norm(side_scale, f32_ref):
        acc = proj(wa_ref[0])
        for c in range(tn // LANES):
            sl = slice(c * LANES, (c + 1) * LANES)
            y = _rms(acc[:, sl], gn_ref[:, sl])
            if f32_ref is not None:
                f32_ref[:, sl] = y
            side_ref[:, sl] = (y if side_scale is None else y * side_scale).astype(BF16)

    @pl.when(j < nqb)
    def _():
        qk_norm(q_scale, None)

    @pl.when((j >= nqb) & (j < jv0))
    def _():
        qk_norm(None, k_ref)

    @pl.when((j >= jv0) & (j < j0))
    def _():
        acc = proj(wa_ref[0])
        v_ref[...] = acc
        side_ref[...] = acc.astype(BF16)
        if emit_vt:
            vt_ref[...] = acc.T.astype(BF16)

    @pl.when((j >= j0) & (j < na))
    def _():
        z_ref[...] = proj(wa_ref[0])

    @pl.when(j >= na)
    def _():
        z_ref[...] = proj(jnp.concatenate([wgm_ref[0, SUBLANES:, :], wgn_ref[0]], axis=0))


def _in_proj(x, g, w_t, n_a, n_g, gn, kv_all, layer, depth, cfg, t_seq=None):
    r, k = x.shape
    qw, vw = cfg.qw, cfg.vw
    n_side = 2 * qw + vw
    n_z = n_a - n_side + n_g
    tm = _tile(r, 1024, SUBLANES)
    tn = _tile(functools.reduce(math.gcd, (qw, vw, n_a, n_g)), 512, LANES)
    nqb, nvb, na, nsb = qw // tn, vw // tn, n_a // tn, n_side // tn
    nrb = r // tm
    assert w_t.shape[1] == n_a + SUBLANES + n_g and n_a % tn == 0
    tiles_per_block = tn // SUBLANES
    in_specs = [
        pl.BlockSpec((tm, k), lambda i, j: (i, 0)),
        pl.BlockSpec((1, k), lambda i, j: (0, 0)),
        pl.BlockSpec((1, tn, k), lambda i, j: (layer, jnp.minimum(j, na - 1), 0)),
        pl.BlockSpec((1, tn, k), lambda i, j: (layer, jnp.maximum(j, na), 0)),
        pl.BlockSpec((1, SUBLANES, k), lambda i, j: (layer, (jnp.maximum(j, na) + 1) * tiles_per_block, 0)),
        pl.BlockSpec((1, tn), lambda i, j: (0, jnp.minimum(j, 2 * nqb - 1))),
    ]
    args = [x, g, w_t, w_t, w_t, gn]
    aliases = {}
    if kv_all is not None:
        in_specs += [pl.BlockSpec(memory_space=pl.ANY), pl.BlockSpec(memory_space=pl.ANY)]
        args += list(kv_all)
        aliases = {6: 2, 7: 3}
    out_specs = [
        pl.BlockSpec((tm, tn), lambda i, j: (i, jnp.maximum(j - nsb, 0))),
        pl.BlockSpec((tm, tn), lambda i, j: (i, jnp.minimum(j, nsb - 1))),
        pl.BlockSpec((tm, tn), lambda i, j: (layer * nrb + i, jnp.clip(j - nqb, 0, nqb - 1))),
        pl.BlockSpec((tm, tn), lambda i, j: (layer * nrb + i, jnp.clip(j - 2 * nqb, 0, nvb - 1))),
    ]
    out_shape = [jax.ShapeDtypeStruct((r, n_z), F32), jax.ShapeDtypeStruct((r, n_side), BF16),
                 jax.ShapeDtypeStruct((depth * r, qw), F32), jax.ShapeDtypeStruct((depth * r, vw), F32)]
    emit_vt = t_seq is not None and t_seq % tm == 0
    if emit_vt:
        tpb = t_seq // tm
        out_specs.append(pl.BlockSpec(
            (tn, tm), lambda i, j: ((i // tpb) * nvb + jnp.clip(j - 2 * nqb, 0, nvb - 1), i % tpb)))
        out_shape.append(jax.ShapeDtypeStruct((r // t_seq * vw, t_seq), BF16))
    return pl.pallas_call(
        functools.partial(_in_proj_kernel, n_in=len(args), emit_vt=emit_vt, nqb=nqb, nvb=nvb, na=na,
                          q_scale=cfg.e ** -0.5 * LOG2E, tn=tn),
        grid=(nrb, na + n_g // tn),
        in_specs=in_specs,
        out_specs=out_specs,
        out_shape=out_shape,
        scratch_shapes=[pltpu.VMEM((tm, k), BF16)],
        input_output_aliases=aliases,
        compiler_params=_cparams("parallel", "arbitrary"),
        name="in_proj",
    )(*args)


def _gates_kernel(x_ref, g_ref, wt_ref, b_ref, bt_ref, o_ref, ot_ref, *, mh):
    hin = _rms(x_ref[...], g_ref[...])
    wt = wt_ref[0]
    wt_rows = jnp.concatenate([wt, jnp.zeros((LANES - SUBLANES, wt.shape[1]), F32)], axis=0)
    gif = _dot3(hin, wt_rows, NT_DIMS) + b_ref[...]
    lane = lax.broadcasted_iota(jnp.int32, gif.shape, 1)
    o_ref[...] = jnp.where(lane < mh, gif, _log_sigmoid(gif))
    gift = _dot3(wt, hin, NT_DIMS) + bt_ref[:, :1]
    sub = lax.broadcasted_iota(jnp.int32, gift.shape, 0)
    ot_ref[0] = jnp.where(sub < mh, gift, _log_sigmoid(gift))


def _gates(x, g, w_t, row0, layer, b_if, bsz, t, cfg):
    r, k = x.shape
    ng = 2 * cfg.mh
    assert ng == SUBLANES and row0 % SUBLANES == 0
    tm = _tile(t, 512, LANES) if t % LANES == 0 else t
    b = jnp.zeros((1, LANES), F32).at[0, :ng].set(b_if)
    bt = jnp.zeros((SUBLANES, LANES), F32).at[:ng].set(jnp.broadcast_to(b_if[:, None], (ng, LANES)))
    nt = t // tm
    return pl.pallas_call(
        functools.partial(_gates_kernel, mh=cfg.mh),
        grid=(bsz, nt),
        in_specs=[
            pl.BlockSpec((tm, k), lambda bi, ti: (bi * nt + ti, 0)),
            pl.BlockSpec((1, k), lambda bi, ti: (0, 0)),
            pl.BlockSpec((1, SUBLANES, k), lambda bi, ti: (layer, row0 // SUBLANES, 0)),
            pl.BlockSpec((1, LANES), lambda bi, ti: (0, 0)),
            pl.BlockSpec((SUBLANES, LANES), lambda bi, ti: (0, 0)),
        ],
        out_specs=[
            pl.BlockSpec((tm, LANES), lambda bi, ti: (bi * nt + ti, 0)),
            pl.BlockSpec((1, SUBLANES, tm), lambda bi, ti: (bi, 0, ti)),
        ],
        out_shape=[jax.ShapeDtypeStruct((r, LANES), F32),
                   jax.ShapeDtypeStruct((bsz, SUBLANES, t), F32)],
        compiler_params=_cparams("parallel", "parallel"),
        name="gates",
    )(x, g, w_t, b, bt)


def _lambda_vec(wl_ref, lam_init):
    wl = wl_ref[...]
    a = jnp.sum(wl[0:1] * wl[1:2], axis=-1, keepdims=True)
    b = jnp.sum(wl[2:3] * wl[3:4], axis=-1, keepdims=True)
    return jnp.exp(a) - jnp.exp(b) + lam_init


def _attn_kernel(qi_ref, ki_ref, pt_ref, q_ref, k_ref, vt_ref, sq_ref, skn_ref, svn_ref, *refs,
                 tq, e, dv, hpb, ah, ts, tpad, ppb, units_per_seq, n_units, steps_per_bh, lam_init):
    del pt_ref
    kp_refs, vp_refs = refs[:ppb], refs[ppb:2 * ppb]
    wl_ref, gs_ref, gsc_ref, o_ref, so_ref, m_s, l_s, acc_s, qs_s, bias_s, sm_s, sl_s, sacc_s = refs[2 * ppb:]
    step = pl.program_id(2)
    qi = qi_ref[step]
    ki = ki_ref[step]
    flat = (pl.program_id(0) * pl.num_programs(1) + pl.program_id(1)) * steps_per_bh + step
    active = flat < n_units
    page_group = lax.rem(jnp.minimum(flat, n_units - 1), units_per_seq)
    decode_init, decode_pages, decode_finish = _decode_parts(
        sq_ref, skn_ref, svn_ref, kp_refs, vp_refs, wl_ref, gs_ref, so_ref, qs_s, bias_s, sm_s, sl_s, sacc_s,
        ah=ah, e=e, dv=dv, ts=ts, tpad=tpad, lam_init=lam_init)

    @pl.when(ki == 0)
    def _():
        m_s[...] = jnp.full(m_s.shape, NEG_INF, F32)
        l_s[...] = jnp.zeros(l_s.shape, F32)
        acc_s[...] = jnp.zeros(acc_s.shape, F32)

    @pl.when(active & (page_group == 0))
    def _():
        decode_init()

    def body(masked):
        if masked:
            key = lax.broadcasted_iota(jnp.int32, (tq, tq), 0)
            qry = lax.broadcasted_iota(jnp.int32, (tq, tq), 1)
            keep = key <= qry
        for hc in range(2 * hpb):
            cols = slice(hc * e, (hc + 1) * e)
            st = lax.dot_general(k_ref[:, cols], q_ref[:, cols], NT_DIMS,
                                 preferred_element_type=F32)
            if masked:
                st = jnp.where(keep, st, NEG_INF)
            m_prev = m_s[hc]
            m_new = jnp.maximum(m_prev, jnp.max(st, axis=0, keepdims=True))
            alpha = jnp.exp2(m_prev - m_new)
            pt = jnp.exp2(st - m_new)
            l_s[hc] = alpha * l_s[hc] + jnp.sum(pt, axis=0, keepdims=True)
            m_s[hc] = m_new
            vt = vt_ref[(hc // 2) * dv:(hc // 2 + 1) * dv]
            acc_s[hc] = alpha * acc_s[hc] + jnp.dot(vt, pt.astype(BF16), preferred_element_type=F32)

    @pl.when(ki < qi)
    def _():
        body(False)
        decode_pages()

    @pl.when(ki == qi)
    def _():
        body(True)
        decode_pages()
        lam = _lambda_vec(wl_ref, lam_init)
        for h in range(hpb):
            ot = (acc_s[2 * h] * (1.0 / l_s[2 * h])
                  - lam * (acc_s[2 * h + 1] * (1.0 / l_s[2 * h + 1])))
            ms = jnp.mean(ot * ot, axis=0, keepdims=True)
            ot = ot * lax.rsqrt(ms + EPS) * gsc_ref[...] * (1.0 - lam_init)
            o_ref[:, h * dv:(h + 1) * dv] = ot.T.astype(o_ref.dtype)

    @pl.when(active & (page_group == units_per_seq - 1))
    def _():
        decode_finish()


def _decode_parts(q_ref, kn_ref, vn_ref, kp_refs, vp_refs, wl_ref, gs_ref, o_ref, qs_s, bias_s, m_s, l_s, acc_s,
                  *, ah, e, dv, ts, tpad, lam_init):
    ppb = len(kp_refs)
    rph = SUBLANES
    page = kp_refs[0].shape[2]
    row8 = lax.broadcasted_iota(jnp.int32, (rph, 2 * e), 0)
    lane = lax.broadcasted_iota(jnp.int32, (rph, 2 * e), 1)

    m_rows = ah * rph

    def row_head(shape):
        return lax.shift_right_logical(lax.broadcasted_iota(jnp.int32, shape, 0), int(math.log2(rph)))

    def init():
        for h in range(ah):
            qh = q_ref[0:rph, h * 2 * e:(h + 1) * 2 * e].astype(F32)
            first = jnp.where((row8 < ts) & (lane < e), qh, 0.0)
            second = jnp.where((row8 >= ts) & (row8 < 2 * ts) & (lane >= e), pltpu.roll(qh, ts, 0), 0.0)
            qs_s[h * rph:(h + 1) * rph] = first + second
        c_head = lax.broadcasted_iota(jnp.int32, bias_s.shape, 1) & (ah - 1)
        bias_s[...] = jnp.where(row_head(bias_s.shape) == c_head, 0.0, NEG_INF)
        m_s[...] = jnp.full(m_s.shape, NEG_INF, F32)
        l_s[...] = jnp.zeros(l_s.shape, F32)
        acc_s[...] = jnp.zeros(acc_s.shape, F32)

    def online_update(scores, pv):
        m_prev = m_s[...]
        m_new = m_prev
        for s in scores:
            m_new = jnp.maximum(m_new, jnp.max(s, axis=-1, keepdims=True))
        alpha = jnp.exp2(m_prev - m_new)
        l_new = alpha * l_s[...]
        acc = alpha * acc_s[...]
        for i, s in enumerate(scores):
            pr = jnp.exp2(s - m_new)
            l_new = l_new + jnp.sum(pr, axis=-1, keepdims=True)
            acc = acc + pv(i, pr)
        l_s[...] = l_new
        m_s[...] = m_new
        acc_s[...] = acc

    def pages():
        qs = qs_s[...].astype(BF16)
        bias = bias_s[...]
        scores = [lax.dot_general(qs, kp_refs[i][0, 0].reshape(page * ah, 2 * e).astype(BF16), NT_DIMS,
                                  preferred_element_type=F32) + bias for i in range(ppb)]
        online_update(scores,
                      lambda i, pr: jnp.dot(pr.astype(BF16), vp_refs[i][0, 0].reshape(page * ah, dv).astype(BF16),
                                            preferred_element_type=F32))

    def finish():
        qs = qs_s[...].astype(BF16)
        rh = row_head((m_rows, tpad))
        row = lax.broadcasted_iota(jnp.int32, (m_rows, tpad), 0) & (rph - 1)
        qt = jnp.where(row < ts, row, row - ts)
        kt = lax.broadcasted_iota(jnp.int32, (m_rows, tpad), 1)
        s_new = jnp.full((m_rows, tpad), NEG_INF, F32)
        for h in range(ah):
            kh = kn_ref[:, (ah + h) * 2 * e:(ah + h + 1) * 2 * e]
            sh = lax.dot_general(qs, kh, NT_DIMS, preferred_element_type=F32)
            s_new = jnp.where((rh == h) & (kt <= qt), sh, s_new)

        def pv_new(_, pr):
            out = jnp.zeros((m_rows, dv), F32)
            for h in range(ah):
                ph = jnp.where(rh == h, pr, 0.0).astype(BF16)
                out = out + jnp.dot(ph, vn_ref[:, h * dv:(h + 1) * dv], preferred_element_type=F32)
            return out

        online_update([s_new], pv_new)

        lam = _lambda_vec(wl_ref, lam_init)
        zeros = jnp.zeros((tpad - rph, dv), F32)
        for h in range(ah):
            rows = slice(h * rph, (h + 1) * rph)
            a = acc_s[rows] * (1.0 / l_s[rows])
            o = a - lam * pltpu.roll(a, rph - ts, 0)
            o = _rms(o, gs_ref[...]) * (1.0 - lam_init)
            o = jnp.where(lax.broadcasted_iota(jnp.int32, o.shape, 0) < ts, o, 0.0)
            o_ref[:, h * dv:(h + 1) * dv] = jnp.concatenate([o, zeros], axis=0).astype(o_ref.dtype)

    return init, pages, finish


def _attention(side_p, vt, side_s, cache_k, cache_v, page_table, layer, wl, gs, bsz, t, ts, cfg, lam_init):
    e, dv, ah = cfg.e, cfg.dv, cfg.ah
    tpad = SAMPLE_TPAD
    tq = _tile(t, 512, LANES)
    nq = t // tq
    pairs = [(qi, ki) for qi in range(nq) for ki in range(qi + 1)]
    qi_tab = jnp.asarray(np.array([p[0] for p in pairs], np.int32))
    ki_tab = jnp.asarray(np.array([p[1] for p in pairs], np.int32))
    if vt is None:
        vt = side_p[:, 2 * cfg.qw:].reshape(bsz, t, ah, dv).transpose(0, 2, 3, 1).reshape(bsz * ah * dv, t)
    hpb = 4 if ah % 4 == 0 else (2 if ah % 2 == 0 else 1)
    nhb = ah // hpb
    kb = cfg.qw // (hpb * 2 * e)
    npairs = len(pairs)
    n_steps = bsz * nhb * npairs

    bs, npages = page_table.shape
    page = cache_k.shape[2]
    assert 2 * ts <= SUBLANES and ah & (ah - 1) == 0
    m_rows = ah * SUBLANES
    ppb = _tile(npages, 8, 1)
    units_per_seq = npages // ppb
    n_units = bs * units_per_seq
    assert n_units <= n_steps, "the decode units must fit in the prompt grid"

    def unit(b, h, s):
        return jnp.minimum((b * nhb + h) * npairs + s, n_units - 1)

    def seq_spec(width, col):
        return pl.BlockSpec((tpad, width), lambda b, h, s, qt, kt, pt: (unit(b, h, s) // units_per_seq, col))

    def page_spec(width, i):
        def index(b, h, s, qt, kt, pt):
            u = unit(b, h, s)
            return (layer, pt[(u // units_per_seq) * npages + (u % units_per_seq) * ppb + i], 0, 0, 0)
        return pl.BlockSpec((1, 1, page, ah, width), index)

    grid_spec = pltpu.PrefetchScalarGridSpec(
        num_scalar_prefetch=3,
        grid=(bsz, nhb, npairs),
        in_specs=[
            pl.BlockSpec((tq, hpb * 2 * e), lambda b, h, s, qt, kt, pt: (b * nq + qt[s], h)),
            pl.BlockSpec((tq, hpb * 2 * e), lambda b, h, s, qt, kt, pt: (b * nq + kt[s], kb + h)),
            pl.BlockSpec((hpb * dv, tq), lambda b, h, s, qt, kt, pt: (b * nhb + h, kt[s])),
            seq_spec(cfg.qw, 0),
            seq_spec(2 * cfg.qw, 0),
            seq_spec(cfg.vw, 2 * cfg.qw // cfg.vw),
            *[page_spec(2 * e, i) for i in range(ppb)],
            *[page_spec(dv, i) for i in range(ppb)],
            pl.BlockSpec((4, e), lambda b, h, s, qt, kt, pt: (0, 0)),
            pl.BlockSpec((1, dv), lambda b, h, s, qt, kt, pt: (0, 0)),
            pl.BlockSpec((dv, 1), lambda b, h, s, qt, kt, pt: (0, 0)),
        ],
        out_specs=[
            pl.BlockSpec((tq, hpb * dv), lambda b, h, s, qt, kt, pt: (b * nq + qt[s], h)),
            seq_spec(cfg.vw, 0),
        ],
        scratch_shapes=[pltpu.VMEM((2 * hpb, 1, tq), F32), pltpu.VMEM((2 * hpb, 1, tq), F32),
                        pltpu.VMEM((2 * hpb, dv, tq), F32),
                        pltpu.VMEM((m_rows, 2 * e), F32), pltpu.VMEM((m_rows, page * ah), F32),
                        pltpu.VMEM((m_rows, 1), F32), pltpu.VMEM((m_rows, 1), F32), pltpu.VMEM((m_rows, dv), F32)],
    )
    return pl.pallas_call(
        functools.partial(_attn_kernel, tq=tq, e=e, dv=dv, hpb=hpb, ah=ah, ts=ts, tpad=tpad, ppb=ppb,
                          units_per_seq=units_per_seq, n_units=n_units, steps_per_bh=npairs, lam_init=lam_init),
        grid_spec=grid_spec,
        out_shape=[jax.ShapeDtypeStruct((side_p.shape[0], cfg.vw), BF16),
                   jax.ShapeDtypeStruct((side_s.shape[0], cfg.vw), BF16)],
        compiler_params=_cparams("arbitrary", "arbitrary", "arbitrary"),
        name="attention",
    )(qi_tab, ki_tab, page_table.reshape(-1), side_p, side_p, vt, side_s, side_s, side_s,
      *([cache_k] * ppb), *([cache_v] * ppb), wl, gs, gs.reshape(dv, 1))


def _mlstm_kernel(mq_ref, mk_ref, mv_ref, mo_ref, g_ref, gt_ref, wq_ref, wk_ref, stq_ref, stk_ref,
                  c0_ref, n0_ref, m0_ref, gm_ref, hm_ref, c_ref, n_ref, m_ref, xq_s, xk_s,
                  *, L, dh, mh, conv, t_valid, t_total):
    h = pl.program_id(1)
    c = pl.program_id(2)
    hb = conv - 1
    top = SUBLANES

    @pl.when(c == 0)
    def _():
        xq_s[top - hb:top] = stq_ref[0]
        xk_s[top - hb:top] = stk_ref[0]
        c_ref[0] = c0_ref[0]
        n_ref[0] = n0_ref[0]
        m_ref[0] = m0_ref[0]

    xq_s[top:top + L] = mq_ref[...]
    xk_s[top:top + L] = mk_ref[...]
    yq = _causal_conv(xq_s, wq_ref, top, hb, L)
    yk = _causal_conv(xk_s, wk_ref, top, hb, L)
    q = yq * _sigmoid(yq)
    k = yk * _sigmoid(yk) * dh ** -0.5
    xq_s[top - hb:top] = xq_s[top + L - hb:top + L]
    xk_s[top - hb:top] = xk_s[top + L - hb:top + L]

    g = g_ref[...]
    lane = lax.broadcasted_iota(jnp.int32, g.shape, 1)
    ic_col = jnp.sum(jnp.where(lane == h, g, 0.0), axis=-1, keepdims=True)
    fc_col = jnp.sum(jnp.where(lane == mh + h, g, 0.0), axis=-1, keepdims=True)
    gt = gt_ref[0]
    sub = lax.broadcasted_iota(jnp.int32, gt.shape, 0)
    ic_row = jnp.sum(jnp.where(sub == h, gt, 0.0), axis=0, keepdims=True)
    fc_row = jnp.sum(jnp.where(sub == mh + h, gt, 0.0), axis=0, keepdims=True)
    if t_valid < t_total:
        t_col = c * L + lax.broadcasted_iota(jnp.int32, (L, 1), 0)
        t_row = c * L + lax.broadcasted_iota(jnp.int32, (1, L), 1)
        ic_col = jnp.where(t_col < t_valid, ic_col, NEG_INF)
        fc_col = jnp.where(t_col < t_valid, fc_col, 0.0)
        ic_row = jnp.where(t_row < t_valid, ic_row, NEG_INF)
        fc_row = jnp.where(t_row < t_valid, fc_row, 0.0)

    row = lax.broadcasted_iota(jnp.int32, (L, L), 0)
    col = lax.broadcasted_iota(jnp.int32, (L, L), 1)
    causal = col <= row
    bcum_col = jnp.sum(jnp.where(causal, fc_row, 0.0), axis=-1, keepdims=True)
    bcum_row = jnp.sum(jnp.where(row <= col, fc_col, 0.0), axis=0, keepdims=True)

    m_prev = m_ref[0][:, :1]
    n_prev = n_ref[0]
    c_prev = c_ref[0]

    dmat = jnp.where(causal, bcum_col - bcum_row + ic_row, NEG_INF)
    inter = bcum_col + m_prev
    m_t = jnp.maximum(inter, jnp.max(dmat, axis=-1, keepdims=True))
    w_intra = jnp.exp(dmat - m_t)
    w_inter = jnp.exp(inter - m_t)

    qb = q.astype(BF16)
    kb = k.astype(BF16)
    vb = mv_ref[...].astype(BF16)
    s = lax.dot_general(qb, kb, NT_DIMS, preferred_element_type=F32) * w_intra
    num = (w_inter * jnp.dot(qb, c_prev.astype(BF16), preferred_element_type=F32)
           + jnp.dot(s.astype(BF16), vb, preferred_element_type=F32))
    den = w_inter * jnp.sum(q * n_prev, axis=-1, keepdims=True) + jnp.sum(s, axis=-1, keepdims=True)
    hc = num * (1.0 / jnp.maximum(jnp.abs(den), jnp.exp(-m_t)))

    m_new = m_t[L - 1:L]
    b_last = bcum_col[L - 1:L]
    w_s = jnp.exp(b_last - bcum_col + ic_col - m_new)
    decay = jnp.exp(b_last + m_prev - m_new)
    kw = k * w_s
    c_ref[0] = decay * c_prev + lax.dot_general(kw.astype(BF16), vb, TN_DIMS, preferred_element_type=F32)
    n_ref[0] = decay * n_prev + jnp.sum(kw, axis=0, keepdims=True)
    m_ref[0] = jnp.broadcast_to(m_new, m_ref.shape[1:])

    hm_ref[...] = (_rms(hc, gm_ref[...]) * _sigmoid(mo_ref[...])).astype(hm_ref.dtype)


def _mlstm(z, g, gt, wqk, st_qk, c0, n0, m0, g_mout, bsz, t, t_valid, cfg):
    r = z.shape[0]
    dh, mh, conv = cfg.dh, cfg.mh, cfg.conv
    L = _tile(t, 256, LANES) if t % LANES == 0 else t
    nc = t // L
    o_mq = 0
    bh = bsz * mh
    hb = conv - 1
    m0b = jnp.broadcast_to(m0.reshape(bh, 1, 1), (bh, 1, LANES))
    outs = pl.pallas_call(
        functools.partial(_mlstm_kernel, L=L, dh=dh, mh=mh, conv=conv, t_valid=t_valid, t_total=t),
        grid=(bsz, mh, nc),
        in_specs=[
            pl.BlockSpec((L, dh), lambda b, h, c: (b * nc + c, o_mq + h)),
            pl.BlockSpec((L, dh), lambda b, h, c: (b * nc + c, o_mq + mh + h)),
            pl.BlockSpec((L, dh), lambda b, h, c: (b * nc + c, o_mq + 2 * mh + h)),
            pl.BlockSpec((L, dh), lambda b, h, c: (b * nc + c, o_mq + 3 * mh + h)),
            pl.BlockSpec((L, LANES), lambda b, h, c: (b * nc + c, 0)),
            pl.BlockSpec((1, SUBLANES, L), lambda b, h, c: (b, 0, c)),
            pl.BlockSpec((conv, dh), lambda b, h, c: (0, h)),
            pl.BlockSpec((conv, dh), lambda b, h, c: (0, mh + h)),
            pl.BlockSpec((1, hb, dh), lambda b, h, c: (b, 0, h)),
            pl.BlockSpec((1, hb, dh), lambda b, h, c: (b, 0, mh + h)),
            pl.BlockSpec((1, dh, dh), lambda b, h, c: (b * mh + h, 0, 0)),
            pl.BlockSpec((1, 1, dh), lambda b, h, c: (b * mh + h, 0, 0)),
            pl.BlockSpec((1, 1, LANES), lambda b, h, c: (b * mh + h, 0, 0)),
            pl.BlockSpec((1, dh), lambda b, h, c: (0, h)),
        ],
        out_specs=[
            pl.BlockSpec((L, dh), lambda b, h, c: (b * nc + c, h)),
            pl.BlockSpec((1, dh, dh), lambda b, h, c: (b * mh + h, 0, 0)),
            pl.BlockSpec((1, 1, dh), lambda b, h, c: (b * mh + h, 0, 0)),
            pl.BlockSpec((1, 1, LANES), lambda b, h, c: (b * mh + h, 0, 0)),
        ],
        out_shape=[
            jax.ShapeDtypeStruct((r, cfg.mw), BF16),
            jax.ShapeDtypeStruct((bh, dh, dh), F32),
            jax.ShapeDtypeStruct((bh, 1, dh), F32),
            jax.ShapeDtypeStruct((bh, 1, LANES), F32),
        ],
        scratch_shapes=[pltpu.VMEM((L + SUBLANES, dh), F32), pltpu.VMEM((L + SUBLANES, dh), F32)],
        compiler_params=_cparams("parallel", "parallel", "arbitrary"),
        name="mlstm",
    )(z, z, z, z, g, gt, wqk, wqk, st_qk, st_qk, c0.reshape(bh, dh, dh), n0.reshape(bh, 1, dh), m0b,
      g_mout.reshape(1, cfg.mw))
    hm, c_new, n_new, m_new = outs
    return (hm, c_new.reshape(bsz, mh, dh, dh), n_new.reshape(bsz, mh, dh), m_new[:, 0, 0].reshape(bsz, mh))


def _gated_dual_kernel(ao_ref, hm_ref, ga_ref, gm_ref, wa_ref, wm_ref, o_ref):
    ya = jnp.dot(ao_ref[...], _weights_bf16(wa_ref), preferred_element_type=F32)
    ym = jnp.dot(hm_ref[...], _weights_bf16(wm_ref), preferred_element_type=F32)
    o_ref[...] = (_sigmoid(ga_ref[...]) * ya + _sigmoid(gm_ref[...]) * ym).astype(o_ref.dtype)


def _gated_dual(ao, hm, z, wa, wm, layer, gate_off):
    r = ao.shape[0]
    n = wa.shape[2]
    tm = _tile(r, 1024, 2 * SUBLANES)
    tn = _tile(n, 512, LANES)
    ga_b = gate_off // tn
    gm_b = (gate_off + n) // tn
    return pl.pallas_call(
        _gated_dual_kernel,
        grid=(r // tm, n // tn),
        in_specs=[
            pl.BlockSpec((tm, ao.shape[1]), lambda i, j: (i, 0)),
            pl.BlockSpec((tm, hm.shape[1]), lambda i, j: (i, 0)),
            pl.BlockSpec((tm, tn), lambda i, j: (i, ga_b + j)),
            pl.BlockSpec((tm, tn), lambda i, j: (i, gm_b + j)),
            _layer_block(layer, wa.shape[1], tn, lambda j: j),
            _layer_block(layer, wm.shape[1], tn, lambda j: j),
        ],
        out_specs=pl.BlockSpec((tm, tn), lambda i, j: (i, j)),
        out_shape=jax.ShapeDtypeStruct((r, n), BF16),
        compiler_params=_cparams("parallel", "arbitrary"),
        name="gated_dual",
    )(ao, hm, z, z, wa, wm)


def _mm_res_kernel(a_ref, w_ref, x_ref, o_ref):
    o_ref[...] = x_ref[...] + jnp.dot(a_ref[...], _weights_bf16(w_ref), preferred_element_type=F32)


def _mm_res(a, w, x, layer, *, tn_pref, name):
    r, k = a.shape
    n = w.shape[2]
    tm = _tile(r, 1024, 2 * SUBLANES)
    tn = _tile(n, tn_pref, LANES)
    return pl.pallas_call(
        _mm_res_kernel,
        grid=(r // tm, n // tn),
        in_specs=[
            pl.BlockSpec((tm, k), lambda i, j: (i, 0)),
            _layer_block(layer, k, tn, lambda j: j),
            pl.BlockSpec((tm, tn), lambda i, j: (i, j)),
        ],
        out_specs=pl.BlockSpec((tm, tn), lambda i, j: (i, j)),
        out_shape=jax.ShapeDtypeStruct((r, n), F32),
        compiler_params=_cparams("parallel", "arbitrary"),
        name=name,
    )(a, w, x)


def _ffn_up_kernel(x_ref, g_ref, wg_ref, wu_ref, st_ref, wc_ref, act_ref, stout_ref, hin_ref, xp_s, tail_s,
                   *, nseq, tp, conv, blocks_per_seq, t_end):
    i = pl.program_id(0)
    j = pl.program_id(1)
    hb = conv - 1
    top = SUBLANES

    @pl.when(j == 0)
    def _():
        hin_ref[...] = _rms(x_ref[...], g_ref[...]).astype(BF16)

    hin = hin_ref[...]
    gate = jnp.dot(hin, _weights_bf16(wg_ref), preferred_element_type=F32)
    up = jnp.dot(hin, _weights_bf16(wu_ref), preferred_element_type=F32)
    for s in range(nseq):
        rows = slice(s * tp, (s + 1) * tp)
        if blocks_per_seq > 1:
            prev = jnp.where((i % blocks_per_seq) == 0, st_ref[s], tail_s[j, top - hb:top])
        else:
            prev = st_ref[s]
        xp_s[top - hb:top] = prev
        xp_s[top:top + tp] = gate[rows]
        y = _causal_conv(xp_s, wc_ref, top, hb, tp)
        act_ref[rows] = (y * _sigmoid(y) * up[rows]).astype(act_ref.dtype)
        stout_ref[0, s] = xp_s[top + t_end - hb:top + t_end]
    if blocks_per_seq > 1:
        tail_s[j, top - hb:top] = xp_s[top + tp - hb:top + tp]


def _ffn_up(x, g, w, layer, st, wc, bsz, t, t_valid, cfg):
    r, k = x.shape
    dff, conv = cfg.dff, cfg.ffconv
    tn = _tile(dff, 512, LANES)
    ncb = dff // tn
    if t >= 256:
        assert t_valid == t
        tm = _tile(t, 1024, 2 * SUBLANES)
        nseq, tp, bps, t_end = 1, tm, t // tm, tm
    else:
        tm, nseq, tp, bps, t_end = r, bsz, t, 1, t_valid
    nrb = r // tm
    act, tails = pl.pallas_call(
        functools.partial(_ffn_up_kernel, nseq=nseq, tp=tp, conv=conv, blocks_per_seq=bps, t_end=t_end),
        grid=(nrb, ncb),
        in_specs=[
            pl.BlockSpec((tm, k), lambda i, j: (i, 0)),
            pl.BlockSpec((1, k), lambda i, j: (0, 0)),
            _layer_block(layer, k, tn, lambda j: j),
            _layer_block(layer, k, tn, lambda j: ncb + j),
            pl.BlockSpec((nseq, conv - 1, tn), lambda i, j: (i // bps if nseq == 1 else 0, 0, j)),
            pl.BlockSpec((conv, tn), lambda i, j: (0, j)),
        ],
        out_specs=[
            pl.BlockSpec((tm, tn), lambda i, j: (i, j)),
            pl.BlockSpec((1, nseq, conv - 1, tn), lambda i, j: (i, 0, 0, j)),
        ],
        out_shape=[jax.ShapeDtypeStruct((r, dff), BF16),
                   jax.ShapeDtypeStruct((nrb, nseq, conv - 1, dff), F32)],
        scratch_shapes=[pltpu.VMEM((tm, k), BF16), pltpu.VMEM((tp + SUBLANES, tn), F32),
                        pltpu.VMEM((ncb, SUBLANES, tn), F32)],
        compiler_params=_cparams("arbitrary", "arbitrary"),
        name="ffn_up",
    )(x, g, w, w, st, wc)
    if nseq == 1:
        return act, tails.reshape(bsz, bps, conv - 1, dff)[:, bps - 1]
    return act, tails[0]


def _layer_pre(x, t, layer, depth, lw, kv_all, want_vt, cfg):
    outs = _in_proj(x, lw['g_mix'], lw['w_t'], lw['n_a'], 2 * cfg.d, lw['g_qk'], kv_all, layer, depth, cfg,
                    t_seq=t if want_vt else None)
    z, side, k_all, v_all = outs[:4]
    return z, side, (k_all, v_all), (outs[4] if len(outs) > 4 else None)


def _layer_post(x, z, ao, bsz, t, t_valid, layer, lw, st_qk, st_ff, c0, n0, m0, cfg):
    mw = cfg.mw
    g, gt = _gates(x, lw['g_mix'], lw['w_t'], lw['n_a'], layer, lw['b_if'], bsz, t, cfg)
    hm, c_new, n_new, m_new = _mlstm(z, g, gt, lw['w_qkconv'], st_qk, c0, n0, m0, lw['g_mout'], bsz, t, t_valid, cfg)
    merged = _gated_dual(ao, hm, z, lw['w_br_attn'], lw['w_br_mlstm'], layer, 4 * mw)
    x1 = _mm_res(merged, lw['w_out'], x, layer, tn_pref=512, name="out_proj")
    act, ff_new = _ffn_up(x1, lw['g_ffn'], lw['w_ff_in'], layer, st_ff, lw['w_ffconv'], bsz, t, t_valid, cfg)
    x2 = _mm_res(act, lw['w_ff_out'], x1, layer, tn_pref=256, name="ffn_down")

    qk_new = z.reshape(bsz, t, -1)[:, t_valid - (cfg.conv - 1):t_valid, :2 * mw]
    return x2, (c_new, n_new, m_new, qk_new, ff_new)


def _layer_weights(l, cfg, w_t, g_mix, g_q, g_k, w_lambda, g_sub, w_qkconv, b_if, g_mout, w_br_attn,
                   w_br_mlstm, w_out, g_ffn, w_ff_in, w_ffconv, w_ff_out):
    qw, vw, mw, d = cfg.qw, cfg.vw, cfg.mw, cfg.d
    ng = 2 * cfg.mh
    reps = qw // cfg.e
    return {
        'g_mix': g_mix[l].reshape(1, d),
        'w_t': w_t,
        'n_a': 2 * qw + vw + 4 * mw,
        'b_if': b_if[l].reshape(ng),
        'g_qk': jnp.concatenate([jnp.tile(g_q[l], reps), jnp.tile(g_k[l], reps)]).reshape(1, 2 * qw),
        'w_lambda': w_lambda[l],
        'g_sub': g_sub[l].reshape(1, cfg.dv),
        'w_qkconv': w_qkconv[l],
        'g_mout': g_mout[l],
        'w_br_attn': w_br_attn,
        'w_br_mlstm': w_br_mlstm,
        'w_out': w_out,
        'g_ffn': g_ffn[l].reshape(1, d),
        'w_ff_in': w_ff_in,
        'w_ffconv': w_ffconv[l],
        'w_ff_out': w_ff_out,
    }


def kernel(x_prompt, x_sample, cache_k, cache_v, page_table, state_C, state_n, state_m, state_qkconv,
           state_ffconv, g_mix, w_in, g_q, g_k, w_lambda, g_sub, w_qkconv, b_if, g_mout, w_br_attn,
           w_br_mlstm, w_out, g_ffn, w_ff_in, w_ffconv, w_ff_out):
    bp, tp_, d = x_prompt.shape
    bs, ts, _ = x_sample.shape
    depth = w_in.shape[0]
    cfg = Cfg(d=d, ah=cache_k.shape[3], e=cache_k.shape[4] // 2, dv=cache_v.shape[4], mh=state_C.shape[2],
              dh=state_C.shape[3], conv=w_qkconv.shape[1], dff=w_ff_out.shape[1], ffconv=w_ffconv.shape[1])
    assert ts <= SAMPLE_TPAD and ts >= max(cfg.conv, cfg.ffconv) - 1

    yp = x_prompt.reshape(bp * tp_, d)
    ys = jnp.zeros((bs, SAMPLE_TPAD, d), F32).at[:, :ts].set(x_sample).reshape(bs * SAMPLE_TPAD, d)
    zero_qk = jnp.zeros((bp, cfg.conv - 1, 2 * cfg.mw), F32)
    zero_ff = jnp.zeros((bp, cfg.ffconv - 1, cfg.dff), F32)
    zero_c = jnp.zeros((bp, cfg.mh, cfg.dh, cfg.dh), F32)
    zero_n = jnp.zeros((bp, cfg.mh, cfg.dh), F32)
    zero_m = jnp.zeros((bp, cfg.mh), F32)

    new_p = [[] for _ in range(5)]
    new_s = [[] for _ in range(5)]
    kv_p = kv_s = None
    w_t = jnp.swapaxes(w_in, 1, 2)
    for l in range(depth):
        lw = _layer_weights(l, cfg, w_t, g_mix, g_q, g_k, w_lambda, g_sub, w_qkconv, b_if, g_mout,
                            w_br_attn, w_br_mlstm, w_out, g_ffn, w_ff_in, w_ffconv, w_ff_out)
        z_p, side_p, kv_p, vt_p = _layer_pre(yp, tp_, l, depth, lw, kv_p, True, cfg)
        z_s, side_s, kv_s, _ = _layer_pre(ys, SAMPLE_TPAD, l, depth, lw, kv_s, False, cfg)
        ao_p, ao_s = _attention(side_p, vt_p, side_s, cache_k, cache_v, page_table, l, lw['w_lambda'], lw['g_sub'],
                                bp, tp_, ts, cfg, 0.8 - 0.6 * math.exp(-0.3 * l))
        yp, sp = _layer_post(yp, z_p, ao_p, bp, tp_, tp_, l, lw, zero_qk, zero_ff, zero_c, zero_n, zero_m, cfg)
        ys, ss = _layer_post(ys, z_s, ao_s, bs, SAMPLE_TPAD, ts, l, lw, state_qkconv[l], state_ffconv[l],
                             state_C[l], state_n[l], state_m[l], cfg)
        for j in range(5):
            new_p[j].append(sp[j])
            new_s[j].append(ss[j])
    outs_p = [jnp.stack(a, axis=0) for a in new_p]
    outs_s = [jnp.stack(a, axis=0) for a in new_s]
    kp = kv_p[0].reshape(depth, bp, tp_, cfg.ah, 2 * cfg.e)
    vp = kv_p[1].reshape(depth, bp, tp_, cfg.ah, cfg.dv)
    ks = kv_s[0].reshape(depth, bs, SAMPLE_TPAD, cfg.ah, 2 * cfg.e)[:, :, :ts]
    vs = kv_s[1].reshape(depth, bs, SAMPLE_TPAD, cfg.ah, cfg.dv)[:, :, :ts]
    y_prompt = yp.reshape(bp, tp_, d)
    y_sample = ys.reshape(bs, SAMPLE_TPAD, d)[:, :ts]
    return (y_prompt, y_sample, kp, vp, *outs_p, ks, vs, *outs_s)
```

```python
import functools
import math
from typing import NamedTuple

import numpy as np
import jax
import jax.numpy as jnp
from jax import lax
from jax.experimental import pallas as pl
from jax.experimental.pallas import tpu as pltpu

F32 = jnp.float32
BF16 = jnp.bfloat16
EPS = 1e-6
NEG_INF = float("-inf")
LOG2E = math.log2(math.e)

LANES = 128
SUBLANES = 8
SAMPLE_TPAD = 16
VMEM_LIMIT = 56 * 2 ** 20

NT_DIMS = (((1,), (1,)), ((), ()))
NN_DIMS = (((1,), (0,)), ((), ()))
TN_DIMS = (((0,), (0,)), ((), ()))


class Cfg(NamedTuple):
    d: int
    ah: int
    e: int
    dv: int
    mh: int
    dh: int
    conv: int
    dff: int
    ffconv: int

    @property
    def qw(self):
        return self.ah * 2 * self.e

    @property
    def vw(self):
        return self.ah * self.dv

    @property
    def mw(self):
        return self.mh * self.dh


def _tile(n, pref, mult):
    t = min(pref, n)
    t -= t % mult
    while t >= mult:
        if n % t == 0:
            return t
        t -= mult
    return n


def _cparams(*sem):
    return pltpu.CompilerParams(dimension_semantics=sem, vmem_limit_bytes=VMEM_LIMIT)


def _sigmoid(x):
    return 1.0 / (1.0 + jnp.exp(-x))


def _log_sigmoid(x):
    return -(jnp.maximum(-x, 0.0) + jnp.log1p(jnp.exp(-jnp.abs(x))))


def _rms(x, g):
    ms = jnp.mean(x * x, axis=-1, keepdims=True)
    return x * lax.rsqrt(ms + EPS) * g


def _split_bf16(a):
    hi = a.astype(BF16)
    lo = (a - hi.astype(F32)).astype(BF16)
    return hi, lo


def _dot3(a, b, dims):
    ah, al = _split_bf16(a)
    bh, bl = _split_bf16(b)
    f = lambda p, q: lax.dot_general(p, q, dims, preferred_element_type=F32)
    return f(ah, bh) + (f(ah, bl) + f(al, bh))


def _weights_bf16(w_ref):
    w = w_ref[0] if len(w_ref.shape) == 3 else w_ref[...]
    return w.astype(BF16)


def _layer_block(layer, k, tn, col_of):
    return pl.BlockSpec((1, k, tn), lambda i, j: (layer, 0, col_of(j)))


def _causal_conv(xp_s, w_ref, top, hb, rows):
    y = None
    for j in range(hb + 1):
        term = xp_s[top - hb + j:top - hb + j + rows] * w_ref[j:j + 1]
        y = term if y is None else y + term
    return y


def _in_proj_kernel(*refs, n_in, emit_vt, nqb, nvb, na, q_scale, tn):
    x_ref, g_ref, wa_ref, wgm_ref, wgn_ref, gn_ref = refs[:6]
    z_ref, side_ref, k_ref, v_ref = refs[n_in:n_in + 4]
    vt_ref = refs[n_in + 4] if emit_vt else None
    hin_ref = refs[-1]
    j = pl.program_id(1)
    jv0 = 2 * nqb
    j0 = jv0 + nvb

    @pl.when(j == 0)
    def _():
        hin_ref[...] = _rms(x_ref[...], g_ref[...]).astype(BF16)

    def proj(w_rows):
        return lax.dot_general(hin_ref[...], w_rows.astype(BF16), NT_DIMS, preferred_element_type=F32)

    def qk_norm(side_scale, f32_ref):
        acc = proj(wa_ref[0])
        for c in range(tn // LANES):
            sl = slice(c * LANES, (c + 1) * LANES)
            y = _rms(acc[:, sl], gn_ref[:, sl])
            if f32_ref is not None:
                f32_ref[:, sl] = y
            side_ref[:, sl] = (y if side_scale is None else y * side_scale).astype(BF16)

    @pl.when(j < nqb)
    def _():
        qk_norm(q_scale, None)

    @pl.when((j >= nqb) & (j < jv0))
    def _():
        qk_norm(None, k_ref)

    @pl.when((j >= jv0) & (j < j0))
    def _():
        acc = proj(wa_ref[0])
        v_ref[...] = acc
        side_ref[...] = acc.astype(BF16)
        if emit_vt:
            vt_ref[...] = acc.T.astype(BF16)

    @pl.when((j >= j0) & (j < na))
    def _():
        z_ref[...] = proj(wa_ref[0])

    @pl.when(j >= na)
    def _():
        z_ref[...] = proj(jnp.concatenate([wgm_ref[0, SUBLANES:, :], wgn_ref[0]], axis=0))


def _in_proj(x, g, w_t, n_a, n_g, gn, kv_all, layer, depth, cfg, t_seq=None):
    r, k = x.shape
    qw, vw = cfg.qw, cfg.vw
    n_side = 2 * qw + vw
    n_z = n_a - n_side + n_g
    tm = _tile(r, 1024, SUBLANES)
    tn = _tile(functools.reduce(math.gcd, (qw, vw, n_a, n_g)), 512, LANES)
    nqb, nvb, na, nsb = qw // tn, vw // tn, n_a // tn, n_side // tn
    nrb = r // tm
    assert w_t.shape[1] == n_a + SUBLANES + n_g and n_a % tn == 0
    tiles_per_block = tn // SUBLANES
    in_specs = [
        pl.BlockSpec((tm, k), lambda i, j: (i, 0)),
        pl.BlockSpec((1, k), lambda i, j: (0, 0)),
        pl.BlockSpec((1, tn, k), lambda i, j: (layer, jnp.minimum(j, na - 1), 0)),
        pl.BlockSpec((1, tn, k), lambda i, j: (layer, jnp.maximum(j, na), 0)),
        pl.BlockSpec((1, SUBLANES, k), lambda i, j: (layer, (jnp.maximum(j, na) + 1) * tiles_per_block, 0)),
        pl.BlockSpec((1, tn), lambda i, j: (0, jnp.minimum(j, 2 * nqb - 1))),
    ]
    args = [x, g, w_t, w_t, w_t, gn]
    aliases = {}
    if kv_all is not None:
        in_specs += [pl.BlockSpec(memory_space=pl.ANY), pl.BlockSpec(memory_space=pl.ANY)]
        args += list(kv_all)
        aliases = {6: 2, 7: 3}
    out_specs = [
        pl.BlockSpec((tm, tn), lambda i, j: (i, jnp.maximum(j - nsb, 0))),
        pl.BlockSpec((tm, tn), lambda i, j: (i, jnp.minimum(j, nsb - 1))),
        pl.BlockSpec((tm, tn), lambda i, j: (layer * nrb + i, jnp.clip(j - nqb, 0, nqb - 1))),
        pl.BlockSpec((tm, tn), lambda i, j: (layer * nrb + i, jnp.clip(j - 2 * nqb, 0, nvb - 1))),
    ]
    out_shape = [jax.ShapeDtypeStruct((r, n_z), F32), jax.ShapeDtypeStruct((r, n_side), BF16),
                 jax.ShapeDtypeStruct((depth * r, qw), F32), jax.ShapeDtypeStruct((depth * r, vw), F32)]
    emit_vt = t_seq is not None and t_seq % tm == 0
    if emit_vt:
        tpb = t_seq // tm
        out_specs.append(pl.BlockSpec(
            (tn, tm), lambda i, j: ((i // tpb) * nvb + jnp.clip(j - 2 * nqb, 0, nvb - 1), i % tpb)))
        out_shape.append(jax.ShapeDtypeStruct((r // t_seq * vw, t_seq), BF16))
    return pl.pallas_call(
        functools.partial(_in_proj_kernel, n_in=len(args), emit_vt=emit_vt, nqb=nqb, nvb=nvb, na=na,
                          q_scale=cfg.e ** -0.5 * LOG2E, tn=tn),
        grid=(nrb, na + n_g // tn),
        in_specs=in_specs,
        out_specs=out_specs,
        out_shape=out_shape,
        scratch_shapes=[pltpu.VMEM((tm, k), BF16)],
        input_output_aliases=aliases,
        compiler_params=_cparams("parallel", "arbitrary"),
        name="in_proj",
    )(*args)


def _gates_kernel(x_ref, g_ref, wt_ref, b_ref, bt_ref, o_ref, ot_ref, *, mh):
    hin = _rms(x_ref[...], g_ref[...])
    wt = wt_ref[0]
    wt_rows = jnp.concatenate([wt, jnp.zeros((LANES - SUBLANES, wt.shape[1]), F32)], axis=0)
    gif = _dot3(hin, wt_rows, NT_DIMS) + b_ref[...]
    lane = lax.broadcasted_iota(jnp.int32, gif.shape, 1)
    o_ref[...] = jnp.where(lane < mh, gif, _log_sigmoid(gif))
    gift = _dot3(wt, hin, NT_DIMS) + bt_ref[:, :1]
    sub = lax.broadcasted_iota(jnp.int32, gift.shape, 0)
    ot_ref[0] = jnp.where(sub < mh, gift, _log_sigmoid(gift))


def _gates(x, g, w_t, row0, layer, b_if, bsz, t, cfg):
    r, k = x.shape
    ng = 2 * cfg.mh
    assert ng == SUBLANES and row0 % SUBLANES == 0
    tm = _tile(t, 512, LANES) if t % LANES == 0 else t
    b = jnp.zeros((1, LANES), F32).at[0, :ng].set(b_if)
    bt = jnp.zeros((SUBLANES, LANES), F32).at[:ng].set(jnp.broadcast_to(b_if[:, None], (ng, LANES)))
    nt = t // tm
    return pl.pallas_call(
        functools.partial(_gates_kernel, mh=cfg.mh),
        grid=(bsz, nt),
        in_specs=[
            pl.BlockSpec((tm, k), lambda bi, ti: (bi * nt + ti, 0)),
            pl.BlockSpec((1, k), lambda bi, ti: (0, 0)),
            pl.BlockSpec((1, SUBLANES, k), lambda bi, ti: (layer, row0 // SUBLANES, 0)),
            pl.BlockSpec((1, LANES), lambda bi, ti: (0, 0)),
            pl.BlockSpec((SUBLANES, LANES), lambda bi, ti: (0, 0)),
        ],
        out_specs=[
            pl.BlockSpec((tm, LANES), lambda bi, ti: (bi * nt + ti, 0)),
            pl.BlockSpec((1, SUBLANES, tm), lambda bi, ti: (bi, 0, ti)),
        ],
        out_shape=[jax.ShapeDtypeStruct((r, LANES), F32),
                   jax.ShapeDtypeStruct((bsz, SUBLANES, t), F32)],
        compiler_params=_cparams("parallel", "parallel"),
        name="gates",
    )(x, g, w_t, b, bt)


def _lambda_vec(wl_ref, lam_init):
    wl = wl_ref[...]
    a = jnp.sum(wl[0:1] * wl[1:2], axis=-1, keepdims=True)
    b = jnp.sum(wl[2:3] * wl[3:4], axis=-1, keepdims=True)
    return jnp.exp(a) - jnp.exp(b) + lam_init


def _attn_kernel(qi_ref, ki_ref, pt_ref, q_ref, k_ref, vt_ref, sq_ref, skn_ref, svn_ref, *refs,
                 tq, e, dv, hpb, ah, ts, tpad, ppb, units_per_seq, n_units, steps_per_bh, lam_init):
    del pt_ref
    kp_refs, vp_refs = refs[:ppb], refs[ppb:2 * ppb]
    wl_ref, gs_ref, gsc_ref, o_ref, so_ref, m_s, l_s, acc_s, qs_s, bias_s, sm_s, sl_s, sacc_s = refs[2 * ppb:]
    step = pl.program_id(2)
    qi = qi_ref[step]
    ki = ki_ref[step]
    flat = (pl.program_id(0) * pl.num_programs(1) + pl.program_id(1)) * steps_per_bh + step
    active = flat < n_units
    page_group = lax.rem(jnp.minimum(flat, n_units - 1), units_per_seq)
    decode_init, decode_pages, decode_finish = _decode_parts(
        sq_ref, skn_ref, svn_ref, kp_refs, vp_refs, wl_ref, gs_ref, so_ref, qs_s, bias_s, sm_s, sl_s, sacc_s,
        ah=ah, e=e, dv=dv, ts=ts, tpad=tpad, lam_init=lam_init)

    @pl.when(ki == 0)
    def _():
        m_s[...] = jnp.full(m_s.shape, NEG_INF, F32)
        l_s[...] = jnp.zeros(l_s.shape, F32)
        acc_s[...] = jnp.zeros(acc_s.shape, F32)

    @pl.when(active & (page_group == 0))
    def _():
        decode_init()

    def body(masked):
        if masked:
            key = lax.broadcasted_iota(jnp.int32, (tq, tq), 0)
            qry = lax.broadcasted_iota(jnp.int32, (tq, tq), 1)
            keep = key <= qry
        for hc in range(2 * hpb):
            cols = slice(hc * e, (hc + 1) * e)
            st = lax.dot_general(k_ref[:, cols], q_ref[:, cols], NT_DIMS,
                                 preferred_element_type=F32)
            if masked:
                st = jnp.where(keep, st, NEG_INF)
            m_prev = m_s[hc]
            m_new = jnp.maximum(m_prev, jnp.max(st, axis=0, keepdims=True))
            alpha = jnp.exp2(m_prev - m_new)
            pt = jnp.exp2(st - m_new)
            l_s[hc] = alpha * l_s[hc] + jnp.sum(pt, axis=0, keepdims=True)
            m_s[hc] = m_new
            vt = vt_ref[(hc // 2) * dv:(hc // 2 + 1) * dv]
            acc_s[hc] = alpha * acc_s[hc] + jnp.dot(vt, pt.astype(BF16), preferred_element_type=F32)

    @pl.when(ki < qi)
    def _():
        body(False)
        decode_pages()

    @pl.when(ki == qi)
    def _():
        body(True)
        decode_pages()
        lam = _lambda_vec(wl_ref, lam_init)
        for h in range(hpb):
            ot = (acc_s[2 * h] * (1.0 / l_s[2 * h])
                  - lam * (acc_s[2 * h + 1] * (1.0 / l_s[2 * h + 1])))
            ms = jnp.mean(ot * ot, axis=0, keepdims=True)
            ot = ot * lax.rsqrt(ms + EPS) * gsc_ref[...] * (1.0 - lam_init)
            o_ref[:, h * dv:(h + 1) * dv] = ot.T.astype(o_ref.dtype)

    @pl.when(active & (page_group == units_per_seq - 1))
    def _():
        decode_finish()


def _decode_parts(q_ref, kn_ref, vn_ref, kp_refs, vp_refs, wl_ref, gs_ref, o_ref, qs_s, bias_s, m_s, l_s, acc_s,
                  *, ah, e, dv, ts, tpad, lam_init):
    ppb = len(kp_refs)
    rph = SUBLANES
    page = kp_refs[0].shape[2]
    row8 = lax.broadcasted_iota(jnp.int32, (rph, 2 * e), 0)
    lane = lax.broadcasted_iota(jnp.int32, (rph, 2 * e), 1)

    m_rows = ah * rph

    def row_head(shape):
        return lax.shift_right_logical(lax.broadcasted_iota(jnp.int32, shape, 0), int(math.log2(rph)))

    def init():
        for h in range(ah):
            qh = q_ref[0:rph, h * 2 * e:(h + 1) * 2 * e].astype(F32)
            first = jnp.where((row8 < ts) & (lane < e), qh, 0.0)
            second = jnp.where((row8 >= ts) & (row8 < 2 * ts) & (lane >= e), pltpu.roll(qh, ts, 0), 0.0)
            qs_s[h * rph:(h + 1) * rph] = first + second
        c_head = lax.broadcasted_iota(jnp.int32, bias_s.shape, 1) & (ah - 1)
        bias_s[...] = jnp.where(row_head(bias_s.shape) == c_head, 0.0, NEG_INF)
        m_s[...] = jnp.full(m_s.shape, NEG_INF, F32)
        l_s[...] = jnp.zeros(l_s.shape, F32)
        acc_s[...] = jnp.zeros(acc_s.shape, F32)

    def online_update(scores, pv):
        m_prev = m_s[...]
        m_new = m_prev
        for s in scores:
            m_new = jnp.maximum(m_new, jnp.max(s, axis=-1, keepdims=True))
        alpha = jnp.exp2(m_prev - m_new)
        l_new = alpha * l_s[...]
        acc = alpha * acc_s[...]
        for i, s in enumerate(scores):
            pr = jnp.exp2(s - m_new)
            l_new = l_new + jnp.sum(pr, axis=-1, keepdims=True)
            acc = acc + pv(i, pr)
        l_s[...] = l_new
        m_s[...] = m_new
        acc_s[...] = acc

    def pages():
        qs = qs_s[...].astype(BF16)
        bias = bias_s[...]
        scores = [lax.dot_general(qs, kp_refs[i][0, 0].reshape(page * ah, 2 * e).astype(BF16), NT_DIMS,
                                  preferred_element_type=F32) + bias for i in range(ppb)]
        online_update(scores,
                      lambda i, pr: jnp.dot(pr.astype(BF16), vp_refs[i][0, 0].reshape(page * ah, dv).astype(BF16),
                                            preferred_element_type=F32))

    def finish():
        qs = qs_s[...].astype(BF16)
        rh = row_head((m_rows, tpad))
        row = lax.broadcasted_iota(jnp.int32, (m_rows, tpad), 0) & (rph - 1)
        qt = jnp.where(row < ts, row, row - ts)
        kt = lax.broadcasted_iota(jnp.int32, (m_rows, tpad), 1)
        s_new = jnp.full((m_rows, tpad), NEG_INF, F32)
        for h in range(ah):
            kh = kn_ref[:, (ah + h) * 2 * e:(ah + h + 1) * 2 * e]
            sh = lax.dot_general(qs, kh, NT_DIMS, preferred_element_type=F32)
            s_new = jnp.where((rh == h) & (kt <= qt), sh, s_new)

        def pv_new(_, pr):
            out = jnp.zeros((m_rows, dv), F32)
            for h in range(ah):
                ph = jnp.where(rh == h, pr, 0.0).astype(BF16)
                out = out + jnp.dot(ph, vn_ref[:, h * dv:(h + 1) * dv], preferred_element_type=F32)
            return out

        online_update([s_new], pv_new)

        lam = _lambda_vec(wl_ref, lam_init)
        zeros = jnp.zeros((tpad - rph, dv), F32)
        for h in range(ah):
            rows = slice(h * rph, (h + 1) * rph)
            a = acc_s[rows] * (1.0 / l_s[rows])
            o = a - lam * pltpu.roll(a, rph - ts, 0)
            o = _rms(o, gs_ref[...]) * (1.0 - lam_init)
            o = jnp.where(lax.broadcasted_iota(jnp.int32, o.shape, 0) < ts, o, 0.0)
            o_ref[:, h * dv:(h + 1) * dv] = jnp.concatenate([o, zeros], axis=0).astype(o_ref.dtype)

    return init, pages, finish


def _attention(side_p, vt, side_s, cache_k, cache_v, page_table, layer, wl, gs, bsz, t, ts, cfg, lam_init):
    e, dv, ah = cfg.e, cfg.dv, cfg.ah
    tpad = SAMPLE_TPAD
    tq = _tile(t, 512, LANES)
    nq = t // tq
    pairs = [(qi, ki) for qi in range(nq) for ki in range(qi + 1)]
    qi_tab = jnp.asarray(np.array([p[0] for p in pairs], np.int32))
    ki_tab = jnp.asarray(np.array([p[1] for p in pairs], np.int32))
    if vt is None:
        vt = side_p[:, 2 * cfg.qw:].reshape(bsz, t, ah, dv).transpose(0, 2, 3, 1).reshape(bsz * ah * dv, t)
    hpb = 4 if ah % 4 == 0 else (2 if ah % 2 == 0 else 1)
    nhb = ah // hpb
    kb = cfg.qw // (hpb * 2 * e)
    npairs = len(pairs)
    n_steps = bsz * nhb * npairs

    bs, npages = page_table.shape
    page = cache_k.shape[2]
    assert 2 * ts <= SUBLANES and ah & (ah - 1) == 0
    m_rows = ah * SUBLANES
    ppb = _tile(npages, 8, 1)
    units_per_seq = npages // ppb
    n_units = bs * units_per_seq
    assert n_units <= n_steps, "the decode units must fit in the prompt grid"

    def unit(b, h, s):
        return jnp.minimum((b * nhb + h) * npairs + s, n_units - 1)

    def seq_spec(width, col):
        return pl.BlockSpec((tpad, width), lambda b, h, s, qt, kt, pt: (unit(b, h, s) // units_per_seq, col))

    def page_spec(width, i):
        def index(b, h, s, qt, kt, pt):
            u = unit(b, h, s)
            return (layer, pt[(u // units_per_seq) * npages + (u % units_per_seq) * ppb + i], 0, 0, 0)
        return pl.BlockSpec((1, 1, page, ah, width), index)

    grid_spec = pltpu.PrefetchScalarGridSpec(
        num_scalar_prefetch=3,
        grid=(bsz, nhb, npairs),
        in_specs=[
            pl.BlockSpec((tq, hpb * 2 * e), lambda b, h, s, qt, kt, pt: (b * nq + qt[s], h)),
            pl.BlockSpec((tq, hpb * 2 * e), lambda b, h, s, qt, kt, pt: (b * nq + kt[s], kb + h)),
            pl.BlockSpec((hpb * dv, tq), lambda b, h, s, qt, kt, pt: (b * nhb + h, kt[s])),
            seq_spec(cfg.qw, 0),
            seq_spec(2 * cfg.qw, 0),
            seq_spec(cfg.vw, 2 * cfg.qw // cfg.vw),
            *[page_spec(2 * e, i) for i in range(ppb)],
            *[page_spec(dv, i) for i in range(ppb)],
            pl.BlockSpec((4, e), lambda b, h, s, qt, kt, pt: (0, 0)),
            pl.BlockSpec((1, dv), lambda b, h, s, qt, kt, pt: (0, 0)),
            pl.BlockSpec((dv, 1), lambda b, h, s, qt, kt, pt: (0, 0)),
        ],
        out_specs=[
            pl.BlockSpec((tq, hpb * dv), lambda b, h, s, qt, kt, pt: (b * nq + qt[s], h)),
            seq_spec(cfg.vw, 0),
        ],
        scratch_shapes=[pltpu.VMEM((2 * hpb, 1, tq), F32), pltpu.VMEM((2 * hpb, 1, tq), F32),
                        pltpu.VMEM((2 * hpb, dv, tq), F32),
                        pltpu.VMEM((m_rows, 2 * e), F32), pltpu.VMEM((m_rows, page * ah), F32),
                        pltpu.VMEM((m_rows, 1), F32), pltpu.VMEM((m_rows, 1), F32), pltpu.VMEM((m_rows, dv), F32)],
    )
    return pl.pallas_call(
        functools.partial(_attn_kernel, tq=tq, e=e, dv=dv, hpb=hpb, ah=ah, ts=ts, tpad=tpad, ppb=ppb,
                          units_per_seq=units_per_seq, n_units=n_units, steps_per_bh=npairs, lam_init=lam_init),
        grid_spec=grid_spec,
        out_shape=[jax.ShapeDtypeStruct((side_p.shape[0], cfg.vw), BF16),
                   jax.ShapeDtypeStruct((side_s.shape[0], cfg.vw), BF16)],
        compiler_params=_cparams("arbitrary", "arbitrary", "arbitrary"),
        name="attention",
    )(qi_tab, ki_tab, page_table.reshape(-1), side_p, side_p, vt, side_s, side_s, side_s,
      *([cache_k] * ppb), *([cache_v] * ppb), wl, gs, gs.reshape(dv, 1))


def _mlstm_kernel(*refs, n_in, L, dh, mh, conv, t_valid, t_total):
    (mq_ref, mk_ref, mv_ref, mo_ref, g_ref, gt_ref, wq_ref, wk_ref, stq_ref, stk_ref,
     c0_ref, n0_ref, m0_ref, gm_ref) = refs[:14]
    hm_ref, c_ref, n_ref, m_ref = refs[n_in:n_in + 4]
    xq_s, xk_s = refs[-2:]
    h = pl.program_id(1)
    c = pl.program_id(2)
    hb = conv - 1
    top = SUBLANES

    @pl.when(c == 0)
    def _():
        xq_s[top - hb:top] = stq_ref[0]
        xk_s[top - hb:top] = stk_ref[0]
        c_ref[0] = c0_ref[0]
        n_ref[0] = n0_ref[0]
        m_ref[0] = m0_ref[0]

    xq_s[top:top + L] = mq_ref[...]
    xk_s[top:top + L] = mk_ref[...]
    yq = _causal_conv(xq_s, wq_ref, top, hb, L)
    yk = _causal_conv(xk_s, wk_ref, top, hb, L)
    q = yq * _sigmoid(yq)
    k = yk * _sigmoid(yk) * dh ** -0.5
    xq_s[top - hb:top] = xq_s[top + L - hb:top + L]
    xk_s[top - hb:top] = xk_s[top + L - hb:top + L]

    g = g_ref[...]
    lane = lax.broadcasted_iota(jnp.int32, g.shape, 1)
    ic_col = jnp.sum(jnp.where(lane == h, g, 0.0), axis=-1, keepdims=True)
    fc_col = jnp.sum(jnp.where(lane == mh + h, g, 0.0), axis=-1, keepdims=True)
    gt = gt_ref[0]
    sub = lax.broadcasted_iota(jnp.int32, gt.shape, 0)
    ic_row = jnp.sum(jnp.where(sub == h, gt, 0.0), axis=0, keepdims=True)
    fc_row = jnp.sum(jnp.where(sub == mh + h, gt, 0.0), axis=0, keepdims=True)
    if t_valid < t_total:
        t_col = c * L + lax.broadcasted_iota(jnp.int32, (L, 1), 0)
        t_row = c * L + lax.broadcasted_iota(jnp.int32, (1, L), 1)
        ic_col = jnp.where(t_col < t_valid, ic_col, NEG_INF)
        fc_col = jnp.where(t_col < t_valid, fc_col, 0.0)
        ic_row = jnp.where(t_row < t_valid, ic_row, NEG_INF)
        fc_row = jnp.where(t_row < t_valid, fc_row, 0.0)

    row = lax.broadcasted_iota(jnp.int32, (L, L), 0)
    col = lax.broadcasted_iota(jnp.int32, (L, L), 1)
    causal = col <= row
    bcum_col = jnp.sum(jnp.where(causal, fc_row, 0.0), axis=-1, keepdims=True)
    bcum_row = jnp.sum(jnp.where(row <= col, fc_col, 0.0), axis=0, keepdims=True)

    m_prev = m_ref[0][:, :1]
    n_prev = n_ref[0]
    c_prev = c_ref[0]

    dmat = jnp.where(causal, bcum_col - bcum_row + ic_row, NEG_INF)
    inter = bcum_col + m_prev
    m_t = jnp.maximum(inter, jnp.max(dmat, axis=-1, keepdims=True))
    w_intra = jnp.exp(dmat - m_t)
    w_inter = jnp.exp(inter - m_t)

    qb = q.astype(BF16)
    kb = k.astype(BF16)
    vb = mv_ref[...].astype(BF16)
    s = lax.dot_general(qb, kb, NT_DIMS, preferred_element_type=F32) * w_intra
    num = (w_inter * jnp.dot(qb, c_prev.astype(BF16), preferred_element_type=F32)
           + jnp.dot(s.astype(BF16), vb, preferred_element_type=F32))
    den = w_inter * jnp.sum(q * n_prev, axis=-1, keepdims=True) + jnp.sum(s, axis=-1, keepdims=True)
    hc = num * (1.0 / jnp.maximum(jnp.abs(den), jnp.exp(-m_t)))

    m_new = m_t[L - 1:L]
    b_last = bcum_col[L - 1:L]
    w_s = jnp.exp(b_last - bcum_col + ic_col - m_new)
    decay = jnp.exp(b_last + m_prev - m_new)
    kw = k * w_s
    c_ref[0] = decay * c_prev + lax.dot_general(kw.astype(BF16), vb, TN_DIMS, preferred_element_type=F32)
    n_ref[0] = decay * n_prev + jnp.sum(kw, axis=0, keepdims=True)
    m_ref[0] = jnp.broadcast_to(m_new, m_ref.shape[1:])

    hm_ref[...] = (_rms(hc, gm_ref[...]) * _sigmoid(mo_ref[...])).astype(hm_ref.dtype)


def _mlstm(z, g, gt, wqk, st_qk, c0, n0, m0, s0, g_mout, c_all, layer, depth, bsz, t, t_valid, cfg):
    r = z.shape[0]
    dh, mh, conv = cfg.dh, cfg.mh, cfg.conv
    L = _tile(t, 256, LANES) if t % LANES == 0 else t
    nc = t // L
    o_mq = 0
    bh = bsz * mh
    hb = conv - 1
    m0b = jnp.broadcast_to(m0.reshape(bh, 1, 1), (bh, 1, LANES))
    args = [z, z, z, z, g, gt, wqk, wqk, st_qk, st_qk, c0, n0, m0b, g_mout.reshape(1, cfg.mw)]
    extra_specs, aliases = [], {}
    if c_all is not None:
        extra_specs = [pl.BlockSpec(memory_space=pl.ANY)]
        aliases = {len(args): 1}
        args.append(c_all)
    outs = pl.pallas_call(
        functools.partial(_mlstm_kernel, n_in=len(args), L=L, dh=dh, mh=mh, conv=conv, t_valid=t_valid, t_total=t),
        grid=(bsz, mh, nc),
        input_output_aliases=aliases,
        in_specs=[
            pl.BlockSpec((L, dh), lambda b, h, c: (b * nc + c, o_mq + h)),
            pl.BlockSpec((L, dh), lambda b, h, c: (b * nc + c, o_mq + mh + h)),
            pl.BlockSpec((L, dh), lambda b, h, c: (b * nc + c, o_mq + 2 * mh + h)),
            pl.BlockSpec((L, dh), lambda b, h, c: (b * nc + c, o_mq + 3 * mh + h)),
            pl.BlockSpec((L, LANES), lambda b, h, c: (b * nc + c, 0)),
            pl.BlockSpec((1, SUBLANES, L), lambda b, h, c: (b, 0, c)),
            pl.BlockSpec((conv, dh), lambda b, h, c: (0, h)),
            pl.BlockSpec((conv, dh), lambda b, h, c: (0, mh + h)),
            pl.BlockSpec((1, hb, dh), lambda b, h, c: (b, 0, h)),
            pl.BlockSpec((1, hb, dh), lambda b, h, c: (b, 0, mh + h)),
            pl.BlockSpec((1, dh, dh), lambda b, h, c: (s0 + b * mh + h, 0, 0)),
            pl.BlockSpec((1, 1, dh), lambda b, h, c: (s0 + b * mh + h, 0, 0)),
            pl.BlockSpec((1, 1, LANES), lambda b, h, c: (b * mh + h, 0, 0)),
            pl.BlockSpec((1, dh), lambda b, h, c: (0, h)),
            *extra_specs,
        ],
        out_specs=[
            pl.BlockSpec((L, dh), lambda b, h, c: (b * nc + c, h)),
            pl.BlockSpec((1, dh, dh), lambda b, h, c: (layer * bh + b * mh + h, 0, 0)),
            pl.BlockSpec((1, 1, dh), lambda b, h, c: (b * mh + h, 0, 0)),
            pl.BlockSpec((1, 1, LANES), lambda b, h, c: (b * mh + h, 0, 0)),
        ],
        out_shape=[
            jax.ShapeDtypeStruct((r, cfg.mw), BF16),
            jax.ShapeDtypeStruct((depth * bh, dh, dh), F32),
            jax.ShapeDtypeStruct((bh, 1, dh), F32),
            jax.ShapeDtypeStruct((bh, 1, LANES), F32),
        ],
        scratch_shapes=[pltpu.VMEM((L + SUBLANES, dh), F32), pltpu.VMEM((L + SUBLANES, dh), F32)],
        compiler_params=_cparams("parallel", "parallel", "arbitrary"),
        name="mlstm",
    )(*args)
    hm, c_all, n_new, m_new = outs
    return hm, c_all, n_new.reshape(bsz, mh, dh), m_new[:, 0, 0].reshape(bsz, mh)


def _gated_dual_kernel(ao_ref, hm_ref, ga_ref, gm_ref, wa_ref, wm_ref, o_ref):
    ya = jnp.dot(ao_ref[...], _weights_bf16(wa_ref), preferred_element_type=F32)
    ym = jnp.dot(hm_ref[...], _weights_bf16(wm_ref), preferred_element_type=F32)
    o_ref[...] = (_sigmoid(ga_ref[...]) * ya + _sigmoid(gm_ref[...]) * ym).astype(o_ref.dtype)


def _gated_dual(ao, hm, z, wa, wm, layer, gate_off):
    r = ao.shape[0]
    n = wa.shape[2]
    tm = _tile(r, 1024, 2 * SUBLANES)
    tn = _tile(n, 512, LANES)
    ga_b = gate_off // tn
    gm_b = (gate_off + n) // tn
    return pl.pallas_call(
        _gated_dual_kernel,
        grid=(r // tm, n // tn),
        in_specs=[
            pl.BlockSpec((tm, ao.shape[1]), lambda i, j: (i, 0)),
            pl.BlockSpec((tm, hm.shape[1]), lambda i, j: (i, 0)),
            pl.BlockSpec((tm, tn), lambda i, j: (i, ga_b + j)),
            pl.BlockSpec((tm, tn), lambda i, j: (i, gm_b + j)),
            _layer_block(layer, wa.shape[1], tn, lambda j: j),
            _layer_block(layer, wm.shape[1], tn, lambda j: j),
        ],
        out_specs=pl.BlockSpec((tm, tn), lambda i, j: (i, j)),
        out_shape=jax.ShapeDtypeStruct((r, n), BF16),
        compiler_params=_cparams("parallel", "arbitrary"),
        name="gated_dual",
    )(ao, hm, z, z, wa, wm)


def _mm_res_kernel(a_ref, w_ref, x_ref, o_ref):
    o_ref[...] = x_ref[...] + jnp.dot(a_ref[...], _weights_bf16(w_ref), preferred_element_type=F32)


def _mm_res(a, w, x, layer, *, tn_pref, name):
    r, k = a.shape
    n = w.shape[2]
    tm = _tile(r, 1024, 2 * SUBLANES)
    tn = _tile(n, tn_pref, LANES)
    return pl.pallas_call(
        _mm_res_kernel,
        grid=(r // tm, n // tn),
        in_specs=[
            pl.BlockSpec((tm, k), lambda i, j: (i, 0)),
            _layer_block(layer, k, tn, lambda j: j),
            pl.BlockSpec((tm, tn), lambda i, j: (i, j)),
        ],
        out_specs=pl.BlockSpec((tm, tn), lambda i, j: (i, j)),
        out_shape=jax.ShapeDtypeStruct((r, n), F32),
        compiler_params=_cparams("parallel", "arbitrary"),
        name=name,
    )(a, w, x)


def _ffn_up_kernel(x_ref, g_ref, wg_ref, wu_ref, st_ref, wc_ref, act_ref, stout_ref, hin_ref, xp_s, tail_s,
                   *, nseq, tp, conv, blocks_per_seq, t_end):
    i = pl.program_id(0)
    j = pl.program_id(1)
    hb = conv - 1
    top = SUBLANES

    @pl.when(j == 0)
    def _():
        hin_ref[...] = _rms(x_ref[...], g_ref[...]).astype(BF16)

    hin = hin_ref[...]
    gate = jnp.dot(hin, _weights_bf16(wg_ref), preferred_element_type=F32)
    up = jnp.dot(hin, _weights_bf16(wu_ref), preferred_element_type=F32)
    for s in range(nseq):
        rows = slice(s * tp, (s + 1) * tp)
        if blocks_per_seq > 1:
            prev = jnp.where((i % blocks_per_seq) == 0, st_ref[s], tail_s[j, top - hb:top])
        else:
            prev = st_ref[s]
        xp_s[top - hb:top] = prev
        xp_s[top:top + tp] = gate[rows]
        y = _causal_conv(xp_s, wc_ref, top, hb, tp)
        act_ref[rows] = (y * _sigmoid(y) * up[rows]).astype(act_ref.dtype)
        stout_ref[0, s] = xp_s[top + t_end - hb:top + t_end]
    if blocks_per_seq > 1:
        tail_s[j, top - hb:top] = xp_s[top + tp - hb:top + tp]


def _ffn_up(x, g, w, layer, st, wc, bsz, t, t_valid, cfg):
    r, k = x.shape
    dff, conv = cfg.dff, cfg.ffconv
    tn = _tile(dff, 512, LANES)
    ncb = dff // tn
    if t >= 256:
        assert t_valid == t
        tm = _tile(t, 1024, 2 * SUBLANES)
        nseq, tp, bps, t_end = 1, tm, t // tm, tm
    else:
        tm, nseq, tp, bps, t_end = r, bsz, t, 1, t_valid
    nrb = r // tm
    act, tails = pl.pallas_call(
        functools.partial(_ffn_up_kernel, nseq=nseq, tp=tp, conv=conv, blocks_per_seq=bps, t_end=t_end),
        grid=(nrb, ncb),
        in_specs=[
            pl.BlockSpec((tm, k), lambda i, j: (i, 0)),
            pl.BlockSpec((1, k), lambda i, j: (0, 0)),
            _layer_block(layer, k, tn, lambda j: j),
            _layer_block(layer, k, tn, lambda j: ncb + j),
            pl.BlockSpec((nseq, conv - 1, tn), lambda i, j: (i // bps if nseq == 1 else 0, 0, j)),
            pl.BlockSpec((conv, tn), lambda i, j: (0, j)),
        ],
        out_specs=[
            pl.BlockSpec((tm, tn), lambda i, j: (i, j)),
            pl.BlockSpec((1, nseq, conv - 1, tn), lambda i, j: (i, 0, 0, j)),
        ],
        out_shape=[jax.ShapeDtypeStruct((r, dff), BF16),
                   jax.ShapeDtypeStruct((nrb, nseq, conv - 1, dff), F32)],
        scratch_shapes=[pltpu.VMEM((tm, k), BF16), pltpu.VMEM((tp + SUBLANES, tn), F32),
                        pltpu.VMEM((ncb, SUBLANES, tn), F32)],
        compiler_params=_cparams("arbitrary", "arbitrary"),
        name="ffn_up",
    )(x, g, w, w, st, wc)
    if nseq == 1:
        return act, tails.reshape(bsz, bps, conv - 1, dff)[:, bps - 1]
    return act, tails[0]


def _layer_pre(x, t, layer, depth, lw, kv_all, want_vt, cfg):
    outs = _in_proj(x, lw['g_mix'], lw['w_t'], lw['n_a'], 2 * cfg.d, lw['g_qk'], kv_all, layer, depth, cfg,
                    t_seq=t if want_vt else None)
    z, side, k_all, v_all = outs[:4]
    return z, side, (k_all, v_all), (outs[4] if len(outs) > 4 else None)


def _layer_post(x, z, ao, bsz, t, t_valid, layer, depth, lw, st_qk, st_ff, c0, n0, m0, s0, c_all, cfg):
    mw = cfg.mw
    g, gt = _gates(x, lw['g_mix'], lw['w_t'], lw['n_a'], layer, lw['b_if'], bsz, t, cfg)
    hm, c_all, n_new, m_new = _mlstm(z, g, gt, lw['w_qkconv'], st_qk, c0, n0, m0, s0, lw['g_mout'], c_all, layer,
                                     depth, bsz, t, t_valid, cfg)
    merged = _gated_dual(ao, hm, z, lw['w_br_attn'], lw['w_br_mlstm'], layer, 4 * mw)
    x1 = _mm_res(merged, lw['w_out'], x, layer, tn_pref=512, name="out_proj")
    act, ff_new = _ffn_up(x1, lw['g_ffn'], lw['w_ff_in'], layer, st_ff, lw['w_ffconv'], bsz, t, t_valid, cfg)
    x2 = _mm_res(act, lw['w_ff_out'], x1, layer, tn_pref=256, name="ffn_down")

    qk_new = z.reshape(bsz, t, -1)[:, t_valid - (cfg.conv - 1):t_valid, :2 * mw]
    return x2, c_all, (n_new, m_new, qk_new, ff_new)


def _layer_weights(l, cfg, w_t, g_mix, g_q, g_k, w_lambda, g_sub, w_qkconv, b_if, g_mout, w_br_attn,
                   w_br_mlstm, w_out, g_ffn, w_ff_in, w_ffconv, w_ff_out):
    qw, vw, mw, d = cfg.qw, cfg.vw, cfg.mw, cfg.d
    ng = 2 * cfg.mh
    reps = qw // cfg.e
    return {
        'g_mix': g_mix[l].reshape(1, d),
        'w_t': w_t,
        'n_a': 2 * qw + vw + 4 * mw,
        'b_if': b_if[l].reshape(ng),
        'g_qk': jnp.concatenate([jnp.tile(g_q[l], reps), jnp.tile(g_k[l], reps)]).reshape(1, 2 * qw),
        'w_lambda': w_lambda[l],
        'g_sub': g_sub[l].reshape(1, cfg.dv),
        'w_qkconv': w_qkconv[l],
        'g_mout': g_mout[l],
        'w_br_attn': w_br_attn,
        'w_br_mlstm': w_br_mlstm,
        'w_out': w_out,
        'g_ffn': g_ffn[l].reshape(1, d),
        'w_ff_in': w_ff_in,
        'w_ffconv': w_ffconv[l],
        'w_ff_out': w_ff_out,
    }


def kernel(x_prompt, x_sample, cache_k, cache_v, page_table, state_C, state_n, state_m, state_qkconv,
           state_ffconv, g_mix, w_in, g_q, g_k, w_lambda, g_sub, w_qkconv, b_if, g_mout, w_br_attn,
           w_br_mlstm, w_out, g_ffn, w_ff_in, w_ffconv, w_ff_out):
    bp, tp_, d = x_prompt.shape
    bs, ts, _ = x_sample.shape
    depth = w_in.shape[0]
    cfg = Cfg(d=d, ah=cache_k.shape[3], e=cache_k.shape[4] // 2, dv=cache_v.shape[4], mh=state_C.shape[2],
              dh=state_C.shape[3], conv=w_qkconv.shape[1], dff=w_ff_out.shape[1], ffconv=w_ffconv.shape[1])
    assert ts <= SAMPLE_TPAD and ts >= max(cfg.conv, cfg.ffconv) - 1

    yp = x_prompt.reshape(bp * tp_, d)
    ys = jnp.zeros((bs, SAMPLE_TPAD, d), F32).at[:, :ts].set(x_sample).reshape(bs * SAMPLE_TPAD, d)
    zero_qk = jnp.zeros((bp, cfg.conv - 1, 2 * cfg.mw), F32)
    zero_ff = jnp.zeros((bp, cfg.ffconv - 1, cfg.dff), F32)
    zero_c = jnp.zeros((bp * cfg.mh, cfg.dh, cfg.dh), F32)
    zero_n = jnp.zeros((bp * cfg.mh, 1, cfg.dh), F32)
    zero_m = jnp.zeros((bp, cfg.mh), F32)
    all_c = state_C.reshape(depth * bs * cfg.mh, cfg.dh, cfg.dh)
    all_n = state_n.reshape(depth * bs * cfg.mh, 1, cfg.dh)

    new_p = [[] for _ in range(4)]
    new_s = [[] for _ in range(4)]
    kv_p = kv_s = c_p = c_s = None
    w_t = jnp.swapaxes(w_in, 1, 2)
    for l in range(depth):
        lw = _layer_weights(l, cfg, w_t, g_mix, g_q, g_k, w_lambda, g_sub, w_qkconv, b_if, g_mout,
                            w_br_attn, w_br_mlstm, w_out, g_ffn, w_ff_in, w_ffconv, w_ff_out)
        z_p, side_p, kv_p, vt_p = _layer_pre(yp, tp_, l, depth, lw, kv_p, True, cfg)
        z_s, side_s, kv_s, _ = _layer_pre(ys, SAMPLE_TPAD, l, depth, lw, kv_s, False, cfg)
        ao_p, ao_s = _attention(side_p, vt_p, side_s, cache_k, cache_v, page_table, l, lw['w_lambda'], lw['g_sub'],
                                bp, tp_, ts, cfg, 0.8 - 0.6 * math.exp(-0.3 * l))
        yp, c_p, sp = _layer_post(yp, z_p, ao_p, bp, tp_, tp_, l, depth, lw, zero_qk, zero_ff, zero_c, zero_n,
                                  zero_m, 0, c_p, cfg)
        ys, c_s, ss = _layer_post(ys, z_s, ao_s, bs, SAMPLE_TPAD, ts, l, depth, lw, state_qkconv[l],
                                  state_ffconv[l], all_c, all_n, state_m[l], l * bs * cfg.mh, c_s, cfg)
        for j in range(4):
            new_p[j].append(sp[j])
            new_s[j].append(ss[j])
    outs_p = [c_p.reshape(depth, bp, cfg.mh, cfg.dh, cfg.dh)] + [jnp.stack(a, axis=0) for a in new_p]
    outs_s = [c_s.reshape(depth, bs, cfg.mh, cfg.dh, cfg.dh)] + [jnp.stack(a, axis=0) for a in new_s]
    kp = kv_p[0].reshape(depth, bp, tp_, cfg.ah, 2 * cfg.e)
    vp = kv_p[1].reshape(depth, bp, tp_, cfg.ah, cfg.dv)
    ks = kv_s[0].reshape(depth, bs, SAMPLE_TPAD, cfg.ah, 2 * cfg.e)[:, :, :ts]
    vs = kv_s[1].reshape(depth, bs, SAMPLE_TPAD, cfg.ah, cfg.dv)[:, :, :ts]
    y_prompt = yp.reshape(bp, tp_, d)
    y_sample = ys.reshape(bs, SAMPLE_TPAD, d)[:, :ts]
    return (y_prompt, y_sample, kp, vp, *outs_p, ks, vs, *outs_s)
```

```python
import functools
import math
from typing import NamedTuple

import numpy as np
import jax
import jax.numpy as jnp
from jax import lax
from jax.experimental import pallas as pl
from jax.experimental.pallas import tpu as pltpu

F32 = jnp.float32
BF16 = jnp.bfloat16
EPS = 1e-6
NEG_INF = float("-inf")
LOG2E = math.log2(math.e)

LANES = 128
SUBLANES = 8
SAMPLE_TPAD = 16
VMEM_LIMIT = 56 * 2 ** 20

NT_DIMS = (((1,), (1,)), ((), ()))
NN_DIMS = (((1,), (0,)), ((), ()))
TN_DIMS = (((0,), (0,)), ((), ()))


class Cfg(NamedTuple):
    d: int
    ah: int
    e: int
    dv: int
    mh: int
    dh: int
    conv: int
    dff: int
    ffconv: int

    @property
    def qw(self):
        return self.ah * 2 * self.e

    @property
    def vw(self):
        return self.ah * self.dv

    @property
    def mw(self):
        return self.mh * self.dh


def _tile(n, pref, mult):
    t = min(pref, n)
    t -= t % mult
    while t >= mult:
        if n % t == 0:
            return t
        t -= mult
    return n


def _cparams(*sem):
    return pltpu.CompilerParams(dimension_semantics=sem, vmem_limit_bytes=VMEM_LIMIT)


def _sigmoid(x):
    return 1.0 / (1.0 + jnp.exp(-x))


def _log_sigmoid(x):
    return -(jnp.maximum(-x, 0.0) + jnp.log1p(jnp.exp(-jnp.abs(x))))


def _rms(x, g):
    ms = jnp.mean(x * x, axis=-1, keepdims=True)
    return x * lax.rsqrt(ms + EPS) * g


def _split_bf16(a):
    hi = a.astype(BF16)
    lo = (a - hi.astype(F32)).astype(BF16)
    return hi, lo


def _dot3(a, b, dims):
    ah, al = _split_bf16(a)
    bh, bl = _split_bf16(b)
    f = lambda p, q: lax.dot_general(p, q, dims, preferred_element_type=F32)
    return f(ah, bh) + (f(ah, bl) + f(al, bh))


def _weights_bf16(w_ref):
    w = w_ref[0] if len(w_ref.shape) == 3 else w_ref[...]
    return w.astype(BF16)


def _layer_block(layer, k, tn, col_of):
    return pl.BlockSpec((1, k, tn), lambda i, j: (layer, 0, col_of(j)))


def _causal_conv(xp_s, w_ref, top, hb, rows):
    y = None
    for j in range(hb + 1):
        term = xp_s[top - hb + j:top - hb + j + rows] * w_ref[j:j + 1]
        y = term if y is None else y + term
    return y


def _in_proj_kernel(*refs, n_in, emit_vt, nqb, nvb, na, q_scale, tn):
    x_ref, g_ref, wa_ref, wgm_ref, wgn_ref, gn_ref = refs[:6]
    z_ref, side_ref, k_ref, v_ref = refs[n_in:n_in + 4]
    vt_ref = refs[n_in + 4] if emit_vt else None
    hin_ref = refs[-1]
    j = pl.program_id(1)
    jv0 = 2 * nqb
    j0 = jv0 + nvb

    @pl.when(j == 0)
    def _():
        hin_ref[...] = _rms(x_ref[...], g_ref[...]).astype(BF16)

    def proj(w_rows):
        return lax.dot_general(hin_ref[...], w_rows.astype(BF16), NT_DIMS, preferred_element_type=F32)

    def qk_norm(side_scale, f32_ref):
        acc = proj(wa_ref[0])
        for c in range(tn // LANES):
            sl = slice(c * LANES, (c + 1) * LANES)
            y = _rms(acc[:, sl], gn_ref[:, sl])
            if f32_ref is not None:
                f32_ref[:, sl] = y
            side_ref[:, sl] = (y if side_scale is None else y * side_scale).astype(BF16)

    @pl.when(j < nqb)
    def _():
        qk_norm(q_scale, None)

    @pl.when((j >= nqb) & (j < jv0))
    def _():
        qk_norm(None, k_ref)

    @pl.when((j >= jv0) & (j < j0))
    def _():
        acc = proj(wa_ref[0])
        v_ref[...] = acc
        side_ref[...] = acc.astype(BF16)
        if emit_vt:
            vt_ref[...] = acc.T.astype(BF16)

    @pl.when((j >= j0) & (j < na))
    def _():
        z_ref[...] = proj(wa_ref[0])

    @pl.when(j >= na)
    def _():
        z_ref[...] = proj(jnp.concatenate([wgm_ref[0, SUBLANES:, :], wgn_ref[0]], axis=0))


def _in_proj(x, g, w_t, n_a, n_g, gn, kv_all, layer, depth, cfg, t_seq=None):
    r, k = x.shape
    qw, vw = cfg.qw, cfg.vw
    n_side = 2 * qw + vw
    n_z = n_a - n_side + n_g
    tm = _tile(r, 1024, SUBLANES)
    tn = _tile(functools.reduce(math.gcd, (qw, vw, n_a, n_g)), 512, LANES)
    nqb, nvb, na, nsb = qw // tn, vw // tn, n_a // tn, n_side // tn
    nrb = r // tm
    assert w_t.shape[1] == n_a + SUBLANES + n_g and n_a % tn == 0
    tiles_per_block = tn // SUBLANES
    in_specs = [
        pl.BlockSpec((tm, k), lambda i, j: (i, 0)),
        pl.BlockSpec((1, k), lambda i, j: (0, 0)),
        pl.BlockSpec((1, tn, k), lambda i, j: (layer, jnp.minimum(j, na - 1), 0)),
        pl.BlockSpec((1, tn, k), lambda i, j: (layer, jnp.maximum(j, na), 0)),
        pl.BlockSpec((1, SUBLANES, k), lambda i, j: (layer, (jnp.maximum(j, na) + 1) * tiles_per_block, 0)),
        pl.BlockSpec((1, tn), lambda i, j: (0, jnp.minimum(j, 2 * nqb - 1))),
    ]
    args = [x, g, w_t, w_t, w_t, gn]
    aliases = {}
    if kv_all is not None:
        in_specs += [pl.BlockSpec(memory_space=pl.ANY), pl.BlockSpec(memory_space=pl.ANY)]
        args += list(kv_all)
        aliases = {6: 2, 7: 3}
    out_specs = [
        pl.BlockSpec((tm, tn), lambda i, j: (i, jnp.maximum(j - nsb, 0))),
        pl.BlockSpec((tm, tn), lambda i, j: (i, jnp.minimum(j, nsb - 1))),
        pl.BlockSpec((tm, tn), lambda i, j: (layer * nrb + i, jnp.clip(j - nqb, 0, nqb - 1))),
        pl.BlockSpec((tm, tn), lambda i, j: (layer * nrb + i, jnp.clip(j - 2 * nqb, 0, nvb - 1))),
    ]
    out_shape = [jax.ShapeDtypeStruct((r, n_z), F32), jax.ShapeDtypeStruct((r, n_side), BF16),
                 jax.ShapeDtypeStruct((depth * r, qw), F32), jax.ShapeDtypeStruct((depth * r, vw), F32)]
    emit_vt = t_seq is not None and t_seq % tm == 0
    if emit_vt:
        tpb = t_seq // tm
        out_specs.append(pl.BlockSpec(
            (tn, tm), lambda i, j: ((i // tpb) * nvb + jnp.clip(j - 2 * nqb, 0, nvb - 1), i % tpb)))
        out_shape.append(jax.ShapeDtypeStruct((r // t_seq * vw, t_seq), BF16))
    return pl.pallas_call(
        functools.partial(_in_proj_kernel, n_in=len(args), emit_vt=emit_vt, nqb=nqb, nvb=nvb, na=na,
                          q_scale=cfg.e ** -0.5 * LOG2E, tn=tn),
        grid=(nrb, na + n_g // tn),
        in_specs=in_specs,
        out_specs=out_specs,
        out_shape=out_shape,
        scratch_shapes=[pltpu.VMEM((tm, k), BF16)],
        input_output_aliases=aliases,
        compiler_params=_cparams("parallel", "arbitrary"),
        name="in_proj",
    )(*args)


def _gates_kernel(x_ref, g_ref, wt_ref, b_ref, bt_ref, o_ref, ot_ref, *, mh):
    hin = _rms(x_ref[...], g_ref[...])
    wt = wt_ref[0]
    wt_rows = jnp.concatenate([wt, jnp.zeros((LANES - SUBLANES, wt.shape[1]), F32)], axis=0)
    gif = _dot3(hin, wt_rows, NT_DIMS) + b_ref[...]
    lane = lax.broadcasted_iota(jnp.int32, gif.shape, 1)
    o_ref[...] = jnp.where(lane < mh, gif, _log_sigmoid(gif))
    gift = _dot3(wt, hin, NT_DIMS) + bt_ref[:, :1]
    sub = lax.broadcasted_iota(jnp.int32, gift.shape, 0)
    ot_ref[0] = jnp.where(sub < mh, gift, _log_sigmoid(gift))


def _gates(x, g, w_t, row0, layer, b_if, bsz, t, cfg):
    r, k = x.shape
    ng = 2 * cfg.mh
    assert ng == SUBLANES and row0 % SUBLANES == 0
    tm = _tile(t, 512, LANES) if t % LANES == 0 else t
    b = jnp.zeros((1, LANES), F32).at[0, :ng].set(b_if)
    bt = jnp.zeros((SUBLANES, LANES), F32).at[:ng].set(jnp.broadcast_to(b_if[:, None], (ng, LANES)))
    nt = t // tm
    return pl.pallas_call(
        functools.partial(_gates_kernel, mh=cfg.mh),
        grid=(bsz, nt),
        in_specs=[
            pl.BlockSpec((tm, k), lambda bi, ti: (bi * nt + ti, 0)),
            pl.BlockSpec((1, k), lambda bi, ti: (0, 0)),
            pl.BlockSpec((1, SUBLANES, k), lambda bi, ti: (layer, row0 // SUBLANES, 0)),
            pl.BlockSpec((1, LANES), lambda bi, ti: (0, 0)),
            pl.BlockSpec((SUBLANES, LANES), lambda bi, ti: (0, 0)),
        ],
        out_specs=[
            pl.BlockSpec((tm, LANES), lambda bi, ti: (bi * nt + ti, 0)),
            pl.BlockSpec((1, SUBLANES, tm), lambda bi, ti: (bi, 0, ti)),
        ],
        out_shape=[jax.ShapeDtypeStruct((r, LANES), F32),
                   jax.ShapeDtypeStruct((bsz, SUBLANES, t), F32)],
        compiler_params=_cparams("parallel", "parallel"),
        name="gates",
    )(x, g, w_t, b, bt)


def _lambda_vec(wl_ref, lam_init):
    wl = wl_ref[...]
    a = jnp.sum(wl[0:1] * wl[1:2], axis=-1, keepdims=True)
    b = jnp.sum(wl[2:3] * wl[3:4], axis=-1, keepdims=True)
    return jnp.exp(a) - jnp.exp(b) + lam_init


def _attn_kernel(qi_ref, ki_ref, pt_ref, q_ref, k_ref, vt_ref, sq_ref, skn_ref, svn_ref, *refs,
                 tq, e, dv, hpb, ah, ts, tpad, ppb, units_per_seq, n_units, steps_per_bh, lam_init):
    del pt_ref
    kp_refs, vp_refs = refs[:ppb], refs[ppb:2 * ppb]
    wl_ref, gs_ref, gsc_ref, o_ref, so_ref, m_s, l_s, acc_s, qs_s, bias_s, sm_s, sl_s, sacc_s = refs[2 * ppb:]
    step = pl.program_id(2)
    qi = qi_ref[step]
    ki = ki_ref[step]
    flat = (pl.program_id(0) * pl.num_programs(1) + pl.program_id(1)) * steps_per_bh + step
    active = flat < n_units
    page_group = lax.rem(jnp.minimum(flat, n_units - 1), units_per_seq)
    decode_init, decode_pages, decode_finish = _decode_parts(
        sq_ref, skn_ref, svn_ref, kp_refs, vp_refs, wl_ref, gs_ref, so_ref, qs_s, bias_s, sm_s, sl_s, sacc_s,
        ah=ah, e=e, dv=dv, ts=ts, tpad=tpad, lam_init=lam_init)

    @pl.when(ki == 0)
    def _():
        m_s[...] = jnp.full(m_s.shape, NEG_INF, F32)
        l_s[...] = jnp.zeros(l_s.shape, F32)
        acc_s[...] = jnp.zeros(acc_s.shape, F32)

    @pl.when(active & (page_group == 0))
    def _():
        decode_init()

    def body(masked):
        if masked:
            key = lax.broadcasted_iota(jnp.int32, (tq, tq), 0)
            qry = lax.broadcasted_iota(jnp.int32, (tq, tq), 1)
            keep = key <= qry
        for hc in range(2 * hpb):
            cols = slice(hc * e, (hc + 1) * e)
            st = lax.dot_general(k_ref[:, cols], q_ref[:, cols], NT_DIMS,
                                 preferred_element_type=F32)
            if masked:
                st = jnp.where(keep, st, NEG_INF)
            m_prev = m_s[hc]
            m_new = jnp.maximum(m_prev, jnp.max(st, axis=0, keepdims=True))
            alpha = jnp.exp2(m_prev - m_new)
            pt = jnp.exp2(st - m_new)
            l_s[hc] = alpha * l_s[hc] + jnp.sum(pt, axis=0, keepdims=True)
            m_s[hc] = m_new
            vt = vt_ref[(hc // 2) * dv:(hc // 2 + 1) * dv]
            acc_s[hc] = alpha * acc_s[hc] + jnp.dot(vt, pt.astype(BF16), preferred_element_type=F32)

    @pl.when(ki < qi)
    def _():
        decode_pages()
        body(False)

    @pl.when(ki == qi)
    def _():
        decode_pages()
        body(True)
        lam = _lambda_vec(wl_ref, lam_init)
        for h in range(hpb):
            ot = (acc_s[2 * h] * (1.0 / l_s[2 * h])
                  - lam * (acc_s[2 * h + 1] * (1.0 / l_s[2 * h + 1])))
            ms = jnp.mean(ot * ot, axis=0, keepdims=True)
            ot = ot * lax.rsqrt(ms + EPS) * gsc_ref[...] * (1.0 - lam_init)
            o_ref[:, h * dv:(h + 1) * dv] = ot.T.astype(o_ref.dtype)

    @pl.when(active & (page_group == units_per_seq - 1))
    def _():
        decode_finish()


def _decode_parts(q_ref, kn_ref, vn_ref, kp_refs, vp_refs, wl_ref, gs_ref, o_ref, qs_s, bias_s, m_s, l_s, acc_s,
                  *, ah, e, dv, ts, tpad, lam_init):
    ppb = len(kp_refs)
    rph = SUBLANES
    page = kp_refs[0].shape[2]
    row8 = lax.broadcasted_iota(jnp.int32, (rph, 2 * e), 0)
    lane = lax.broadcasted_iota(jnp.int32, (rph, 2 * e), 1)

    m_rows = ah * rph

    def row_head(shape):
        return lax.shift_right_logical(lax.broadcasted_iota(jnp.int32, shape, 0), int(math.log2(rph)))

    def init():
        for h in range(ah):
            qh = q_ref[0:rph, h * 2 * e:(h + 1) * 2 * e].astype(F32)
            first = jnp.where((row8 < ts) & (lane < e), qh, 0.0)
            second = jnp.where((row8 >= ts) & (row8 < 2 * ts) & (lane >= e), pltpu.roll(qh, ts, 0), 0.0)
            qs_s[h * rph:(h + 1) * rph] = first + second
        c_head = lax.broadcasted_iota(jnp.int32, bias_s.shape, 1) & (ah - 1)
        bias_s[...] = jnp.where(row_head(bias_s.shape) == c_head, 0.0, NEG_INF)
        m_s[...] = jnp.full(m_s.shape, NEG_INF, F32)
        l_s[...] = jnp.zeros(l_s.shape, F32)
        acc_s[...] = jnp.zeros(acc_s.shape, F32)

    def online_update(scores, pv):
        m_prev = m_s[...]
        m_new = m_prev
        for s in scores:
            m_new = jnp.maximum(m_new, jnp.max(s, axis=-1, keepdims=True))
        alpha = jnp.exp2(m_prev - m_new)
        l_new = alpha * l_s[...]
        acc = alpha * acc_s[...]
        for i, s in enumerate(scores):
            pr = jnp.exp2(s - m_new)
            l_new = l_new + jnp.sum(pr, axis=-1, keepdims=True)
            acc = acc + pv(i, pr)
        l_s[...] = l_new
        m_s[...] = m_new
        acc_s[...] = acc

    def pages():
        qs = qs_s[...].astype(BF16)
        bias = bias_s[...]
        scores = [lax.dot_general(qs, kp_refs[i][0, 0].reshape(page * ah, 2 * e).astype(BF16), NT_DIMS,
                                  preferred_element_type=F32) + bias for i in range(ppb)]
        online_update(scores,
                      lambda i, pr: jnp.dot(pr.astype(BF16), vp_refs[i][0, 0].reshape(page * ah, dv).astype(BF16),
                                            preferred_element_type=F32))

    def finish():
        qs = qs_s[...].astype(BF16)
        rh = row_head((m_rows, tpad))
        row = lax.broadcasted_iota(jnp.int32, (m_rows, tpad), 0) & (rph - 1)
        qt = jnp.where(row < ts, row, row - ts)
        kt = lax.broadcasted_iota(jnp.int32, (m_rows, tpad), 1)
        s_new = jnp.full((m_rows, tpad), NEG_INF, F32)
        for h in range(ah):
            kh = kn_ref[:, (ah + h) * 2 * e:(ah + h + 1) * 2 * e]
            sh = lax.dot_general(qs, kh, NT_DIMS, preferred_element_type=F32)
            s_new = jnp.where((rh == h) & (kt <= qt), sh, s_new)

        def pv_new(_, pr):
            out = jnp.zeros((m_rows, dv), F32)
            for h in range(ah):
                ph = jnp.where(rh == h, pr, 0.0).astype(BF16)
                out = out + jnp.dot(ph, vn_ref[:, h * dv:(h + 1) * dv], preferred_element_type=F32)
            return out

        online_update([s_new], pv_new)

        lam = _lambda_vec(wl_ref, lam_init)
        zeros = jnp.zeros((tpad - rph, dv), F32)
        for h in range(ah):
            rows = slice(h * rph, (h + 1) * rph)
            a = acc_s[rows] * (1.0 / l_s[rows])
            o = a - lam * pltpu.roll(a, rph - ts, 0)
            o = _rms(o, gs_ref[...]) * (1.0 - lam_init)
            o = jnp.where(lax.broadcasted_iota(jnp.int32, o.shape, 0) < ts, o, 0.0)
            o_ref[:, h * dv:(h + 1) * dv] = jnp.concatenate([o, zeros], axis=0).astype(o_ref.dtype)

    return init, pages, finish


def _attention(side_p, vt, side_s, cache_k, cache_v, page_table, layer, wl, gs, bsz, t, ts, cfg, lam_init):
    e, dv, ah = cfg.e, cfg.dv, cfg.ah
    tpad = SAMPLE_TPAD
    tq = _tile(t, 512, LANES)
    nq = t // tq
    pairs = [(qi, ki) for qi in range(nq) for ki in range(qi + 1)]
    qi_tab = jnp.asarray(np.array([p[0] for p in pairs], np.int32))
    ki_tab = jnp.asarray(np.array([p[1] for p in pairs], np.int32))
    if vt is None:
        vt = side_p[:, 2 * cfg.qw:].reshape(bsz, t, ah, dv).transpose(0, 2, 3, 1).reshape(bsz * ah * dv, t)
    hpb = 4 if ah % 4 == 0 else (2 if ah % 2 == 0 else 1)
    nhb = ah // hpb
    kb = cfg.qw // (hpb * 2 * e)
    npairs = len(pairs)
    n_steps = bsz * nhb * npairs

    bs, npages = page_table.shape
    page = cache_k.shape[2]
    assert 2 * ts <= SUBLANES and ah & (ah - 1) == 0
    m_rows = ah * SUBLANES
    ppb = _tile(npages, 8, 1)
    units_per_seq = npages // ppb
    n_units = bs * units_per_seq
    assert n_units <= n_steps, "the decode units must fit in the prompt grid"

    def unit(b, h, s):
        return jnp.minimum((b * nhb + h) * npairs + s, n_units - 1)

    def seq_spec(width, col):
        return pl.BlockSpec((tpad, width), lambda b, h, s, qt, kt, pt: (unit(b, h, s) // units_per_seq, col))

    def page_spec(width, i):
        def index(b, h, s, qt, kt, pt):
            u = unit(b, h, s)
            return (layer, pt[(u // units_per_seq) * npages + (u % units_per_seq) * ppb + i], 0, 0, 0)
        return pl.BlockSpec((1, 1, page, ah, width), index)

    grid_spec = pltpu.PrefetchScalarGridSpec(
        num_scalar_prefetch=3,
        grid=(bsz, nhb, npairs),
        in_specs=[
            pl.BlockSpec((tq, hpb * 2 * e), lambda b, h, s, qt, kt, pt: (b * nq + qt[s], h)),
            pl.BlockSpec((tq, hpb * 2 * e), lambda b, h, s, qt, kt, pt: (b * nq + kt[s], kb + h)),
            pl.BlockSpec((hpb * dv, tq), lambda b, h, s, qt, kt, pt: (b * nhb + h, kt[s])),
            seq_spec(cfg.qw, 0),
            seq_spec(2 * cfg.qw, 0),
            seq_spec(cfg.vw, 2 * cfg.qw // cfg.vw),
            *[page_spec(2 * e, i) for i in range(ppb)],
            *[page_spec(dv, i) for i in range(ppb)],
            pl.BlockSpec((4, e), lambda b, h, s, qt, kt, pt: (0, 0)),
            pl.BlockSpec((1, dv), lambda b, h, s, qt, kt, pt: (0, 0)),
            pl.BlockSpec((dv, 1), lambda b, h, s, qt, kt, pt: (0, 0)),
        ],
        out_specs=[
            pl.BlockSpec((tq, hpb * dv), lambda b, h, s, qt, kt, pt: (b * nq + qt[s], h)),
            seq_spec(cfg.vw, 0),
        ],
        scratch_shapes=[pltpu.VMEM((2 * hpb, 1, tq), F32), pltpu.VMEM((2 * hpb, 1, tq), F32),
                        pltpu.VMEM((2 * hpb, dv, tq), F32),
                        pltpu.VMEM((m_rows, 2 * e), F32), pltpu.VMEM((m_rows, page * ah), F32),
                        pltpu.VMEM((m_rows, 1), F32), pltpu.VMEM((m_rows, 1), F32), pltpu.VMEM((m_rows, dv), F32)],
    )
    return pl.pallas_call(
        functools.partial(_attn_kernel, tq=tq, e=e, dv=dv, hpb=hpb, ah=ah, ts=ts, tpad=tpad, ppb=ppb,
                          units_per_seq=units_per_seq, n_units=n_units, steps_per_bh=npairs, lam_init=lam_init),
        grid_spec=grid_spec,
        out_shape=[jax.ShapeDtypeStruct((side_p.shape[0], cfg.vw), BF16),
                   jax.ShapeDtypeStruct((side_s.shape[0], cfg.vw), BF16)],
        compiler_params=_cparams("arbitrary", "arbitrary", "arbitrary"),
        name="attention",
    )(qi_tab, ki_tab, page_table.reshape(-1), side_p, side_p, vt, side_s, side_s, side_s,
      *([cache_k] * ppb), *([cache_v] * ppb), wl, gs, gs.reshape(dv, 1))


def _mlstm_kernel(*refs, n_in, L, dh, mh, conv, t_valid, t_total):
    (mq_ref, mk_ref, mv_ref, mo_ref, g_ref, gt_ref, wq_ref, wk_ref, stq_ref, stk_ref,
     c0_ref, n0_ref, m0_ref, gm_ref) = refs[:14]
    hm_ref, c_ref, n_ref, m_ref = refs[n_in:n_in + 4]
    xq_s, xk_s = refs[-2:]
    h = pl.program_id(1)
    c = pl.program_id(2)
    hb = conv - 1
    top = SUBLANES

    @pl.when(c == 0)
    def _():
        xq_s[top - hb:top] = stq_ref[0]
        xk_s[top - hb:top] = stk_ref[0]
        c_ref[0] = c0_ref[0]
        n_ref[0] = n0_ref[0]
        m_ref[0] = m0_ref[0]

    xq_s[top:top + L] = mq_ref[...]
    xk_s[top:top + L] = mk_ref[...]
    yq = _causal_conv(xq_s, wq_ref, top, hb, L)
    yk = _causal_conv(xk_s, wk_ref, top, hb, L)
    q = yq * _sigmoid(yq)
    k = yk * _sigmoid(yk) * dh ** -0.5
    xq_s[top - hb:top] = xq_s[top + L - hb:top + L]
    xk_s[top - hb:top] = xk_s[top + L - hb:top + L]

    g = g_ref[...]
    lane = lax.broadcasted_iota(jnp.int32, g.shape, 1)
    ic_col = jnp.sum(jnp.where(lane == h, g, 0.0), axis=-1, keepdims=True)
    fc_col = jnp.sum(jnp.where(lane == mh + h, g, 0.0), axis=-1, keepdims=True)
    gt = gt_ref[0]
    sub = lax.broadcasted_iota(jnp.int32, gt.shape, 0)
    ic_row = jnp.sum(jnp.where(sub == h, gt, 0.0), axis=0, keepdims=True)
    fc_row = jnp.sum(jnp.where(sub == mh + h, gt, 0.0), axis=0, keepdims=True)
    if t_valid < t_total:
        t_col = c * L + lax.broadcasted_iota(jnp.int32, (L, 1), 0)
        t_row = c * L + lax.broadcasted_iota(jnp.int32, (1, L), 1)
        ic_col = jnp.where(t_col < t_valid, ic_col, NEG_INF)
        fc_col = jnp.where(t_col < t_valid, fc_col, 0.0)
        ic_row = jnp.where(t_row < t_valid, ic_row, NEG_INF)
        fc_row = jnp.where(t_row < t_valid, fc_row, 0.0)

    row = lax.broadcasted_iota(jnp.int32, (L, L), 0)
    col = lax.broadcasted_iota(jnp.int32, (L, L), 1)
    causal = col <= row
    bcum_col = jnp.sum(jnp.where(causal, fc_row, 0.0), axis=-1, keepdims=True)
    bcum_row = jnp.sum(jnp.where(row <= col, fc_col, 0.0), axis=0, keepdims=True)

    m_prev = m_ref[0][:, :1]
    n_prev = n_ref[0]
    c_prev = c_ref[0]

    dmat = jnp.where(causal, bcum_col - bcum_row + ic_row, NEG_INF)
    inter = bcum_col + m_prev
    m_t = jnp.maximum(inter, jnp.max(dmat, axis=-1, keepdims=True))
    w_intra = jnp.exp(dmat - m_t)
    w_inter = jnp.exp(inter - m_t)

    qb = q.astype(BF16)
    kb = k.astype(BF16)
    vb = mv_ref[...].astype(BF16)
    s = lax.dot_general(qb, kb, NT_DIMS, preferred_element_type=F32) * w_intra
    num = (w_inter * jnp.dot(qb, c_prev.astype(BF16), preferred_element_type=F32)
           + jnp.dot(s.astype(BF16), vb, preferred_element_type=F32))
    den = w_inter * jnp.sum(q * n_prev, axis=-1, keepdims=True) + jnp.sum(s, axis=-1, keepdims=True)
    hc = num * (1.0 / jnp.maximum(jnp.abs(den), jnp.exp(-m_t)))

    m_new = m_t[L - 1:L]
    b_last = bcum_col[L - 1:L]
    w_s = jnp.exp(b_last - bcum_col + ic_col - m_new)
    decay = jnp.exp(b_last + m_prev - m_new)
    kw = k * w_s
    c_ref[0] = decay * c_prev + lax.dot_general(kw.astype(BF16), vb, TN_DIMS, preferred_element_type=F32)
    n_ref[0] = decay * n_prev + jnp.sum(kw, axis=0, keepdims=True)
    m_ref[0] = jnp.broadcast_to(m_new, m_ref.shape[1:])

    hm_ref[...] = (_rms(hc, gm_ref[...]) * _sigmoid(mo_ref[...])).astype(hm_ref.dtype)


def _mlstm(z, g, gt, wqk, st_qk, c0, n0, m0, s0, g_mout, c_all, layer, depth, bsz, t, t_valid, cfg):
    r = z.shape[0]
    dh, mh, conv = cfg.dh, cfg.mh, cfg.conv
    L = _tile(t, 256, LANES) if t % LANES == 0 else t
    nc = t // L
    o_mq = 0
    bh = bsz * mh
    hb = conv - 1
    m0b = jnp.broadcast_to(m0.reshape(bh, 1, 1), (bh, 1, LANES))
    args = [z, z, z, z, g, gt, wqk, wqk, st_qk, st_qk, c0, n0, m0b, g_mout.reshape(1, cfg.mw)]
    extra_specs, aliases = [], {}
    if c_all is not None:
        extra_specs = [pl.BlockSpec(memory_space=pl.ANY)]
        aliases = {len(args): 1}
        args.append(c_all)
    outs = pl.pallas_call(
        functools.partial(_mlstm_kernel, n_in=len(args), L=L, dh=dh, mh=mh, conv=conv, t_valid=t_valid, t_total=t),
        grid=(bsz, mh, nc),
        input_output_aliases=aliases,
        in_specs=[
            pl.BlockSpec((L, dh), lambda b, h, c: (b * nc + c, o_mq + h)),
            pl.BlockSpec((L, dh), lambda b, h, c: (b * nc + c, o_mq + mh + h)),
            pl.BlockSpec((L, dh), lambda b, h, c: (b * nc + c, o_mq + 2 * mh + h)),
            pl.BlockSpec((L, dh), lambda b, h, c: (b * nc + c, o_mq + 3 * mh + h)),
            pl.BlockSpec((L, LANES), lambda b, h, c: (b * nc + c, 0)),
            pl.BlockSpec((1, SUBLANES, L), lambda b, h, c: (b, 0, c)),
            pl.BlockSpec((conv, dh), lambda b, h, c: (0, h)),
            pl.BlockSpec((conv, dh), lambda b, h, c: (0, mh + h)),
            pl.BlockSpec((1, hb, dh), lambda b, h, c: (b, 0, h)),
            pl.BlockSpec((1, hb, dh), lambda b, h, c: (b, 0, mh + h)),
            pl.BlockSpec((1, dh, dh), lambda b, h, c: (s0 + b * mh + h, 0, 0)),
            pl.BlockSpec((1, 1, dh), lambda b, h, c: (s0 + b * mh + h, 0, 0)),
            pl.BlockSpec((1, 1, LANES), lambda b, h, c: (b * mh + h, 0, 0)),
            pl.BlockSpec((1, dh), lambda b, h, c: (0, h)),
            *extra_specs,
        ],
        out_specs=[
            pl.BlockSpec((L, dh), lambda b, h, c: (b * nc + c, h)),
            pl.BlockSpec((1, dh, dh), lambda b, h, c: (layer * bh + b * mh + h, 0, 0)),
            pl.BlockSpec((1, 1, dh), lambda b, h, c: (b * mh + h, 0, 0)),
            pl.BlockSpec((1, 1, LANES), lambda b, h, c: (b * mh + h, 0, 0)),
        ],
        out_shape=[
            jax.ShapeDtypeStruct((r, cfg.mw), BF16),
            jax.ShapeDtypeStruct((depth * bh, dh, dh), F32),
            jax.ShapeDtypeStruct((bh, 1, dh), F32),
            jax.ShapeDtypeStruct((bh, 1, LANES), F32),
        ],
        scratch_shapes=[pltpu.VMEM((L + SUBLANES, dh), F32), pltpu.VMEM((L + SUBLANES, dh), F32)],
        compiler_params=_cparams("parallel", "parallel", "arbitrary"),
        name="mlstm",
    )(*args)
    hm, c_all, n_new, m_new = outs
    return hm, c_all, n_new.reshape(bsz, mh, dh), m_new[:, 0, 0].reshape(bsz, mh)


def _gated_dual_kernel(ao_ref, hm_ref, ga_ref, gm_ref, wa_ref, wm_ref, o_ref):
    ya = jnp.dot(ao_ref[...], _weights_bf16(wa_ref), preferred_element_type=F32)
    ym = jnp.dot(hm_ref[...], _weights_bf16(wm_ref), preferred_element_type=F32)
    o_ref[...] = (_sigmoid(ga_ref[...]) * ya + _sigmoid(gm_ref[...]) * ym).astype(o_ref.dtype)


def _gated_dual(ao, hm, z, wa, wm, layer, gate_off):
    r = ao.shape[0]
    n = wa.shape[2]
    tm = _tile(r, 1024, 2 * SUBLANES)
    tn = _tile(n, 512, LANES)
    ga_b = gate_off // tn
    gm_b = (gate_off + n) // tn
    return pl.pallas_call(
        _gated_dual_kernel,
        grid=(r // tm, n // tn),
        in_specs=[
            pl.BlockSpec((tm, ao.shape[1]), lambda i, j: (i, 0)),
            pl.BlockSpec((tm, hm.shape[1]), lambda i, j: (i, 0)),
            pl.BlockSpec((tm, tn), lambda i, j: (i, ga_b + j)),
            pl.BlockSpec((tm, tn), lambda i, j: (i, gm_b + j)),
            _layer_block(layer, wa.shape[1], tn, lambda j: j),
            _layer_block(layer, wm.shape[1], tn, lambda j: j),
        ],
        out_specs=pl.BlockSpec((tm, tn), lambda i, j: (i, j)),
        out_shape=jax.ShapeDtypeStruct((r, n), BF16),
        compiler_params=_cparams("parallel", "arbitrary"),
        name="gated_dual",
    )(ao, hm, z, z, wa, wm)


def _mm_res_kernel(a_ref, w_ref, x_ref, o_ref):
    o_ref[...] = x_ref[...] + jnp.dot(a_ref[...], _weights_bf16(w_ref), preferred_element_type=F32)


def _mm_res(a, w, x, layer, *, tn_pref, name):
    r, k = a.shape
    n = w.shape[2]
    tm = _tile(r, 1024, 2 * SUBLANES)
    tn = _tile(n, tn_pref, LANES)
    return pl.pallas_call(
        _mm_res_kernel,
        grid=(r // tm, n // tn),
        in_specs=[
            pl.BlockSpec((tm, k), lambda i, j: (i, 0)),
            _layer_block(layer, k, tn, lambda j: j),
            pl.BlockSpec((tm, tn), lambda i, j: (i, j)),
        ],
        out_specs=pl.BlockSpec((tm, tn), lambda i, j: (i, j)),
        out_shape=jax.ShapeDtypeStruct((r, n), F32),
        compiler_params=_cparams("parallel", "arbitrary"),
        name=name,
    )(a, w, x)


def _ffn_up_kernel(x_ref, g_ref, wg_ref, wu_ref, st_ref, wc_ref, act_ref, stout_ref, hin_ref, xp_s, tail_s,
                   *, nseq, tp, conv, blocks_per_seq, t_end):
    i = pl.program_id(0)
    j = pl.program_id(1)
    hb = conv - 1
    top = SUBLANES

    @pl.when(j == 0)
    def _():
        hin_ref[...] = _rms(x_ref[...], g_ref[...]).astype(BF16)

    hin = hin_ref[...]
    gate = jnp.dot(hin, _weights_bf16(wg_ref), preferred_element_type=F32)
    up = jnp.dot(hin, _weights_bf16(wu_ref), preferred_element_type=F32)
    for s in range(nseq):
        rows = slice(s * tp, (s + 1) * tp)
        if blocks_per_seq > 1:
            prev = jnp.where((i % blocks_per_seq) == 0, st_ref[s], tail_s[j, top - hb:top])
        else:
            prev = st_ref[s]
        xp_s[top - hb:top] = prev
        xp_s[top:top + tp] = gate[rows]
        y = _causal_conv(xp_s, wc_ref, top, hb, tp)
        act_ref[rows] = (y * _sigmoid(y) * up[rows]).astype(act_ref.dtype)
        stout_ref[0, s] = xp_s[top + t_end - hb:top + t_end]
    if blocks_per_seq > 1:
        tail_s[j, top - hb:top] = xp_s[top + tp - hb:top + tp]


def _ffn_up(x, g, w, layer, st, wc, bsz, t, t_valid, cfg):
    r, k = x.shape
    dff, conv = cfg.dff, cfg.ffconv
    tn = _tile(dff, 512, LANES)
    ncb = dff // tn
    if t >= 256:
        assert t_valid == t
        tm = _tile(t, 1024, 2 * SUBLANES)
        nseq, tp, bps, t_end = 1, tm, t // tm, tm
    else:
        tm, nseq, tp, bps, t_end = r, bsz, t, 1, t_valid
    nrb = r // tm
    act, tails = pl.pallas_call(
        functools.partial(_ffn_up_kernel, nseq=nseq, tp=tp, conv=conv, blocks_per_seq=bps, t_end=t_end),
        grid=(nrb, ncb),
        in_specs=[
            pl.BlockSpec((tm, k), lambda i, j: (i, 0)),
            pl.BlockSpec((1, k), lambda i, j: (0, 0)),
            _layer_block(layer, k, tn, lambda j: j),
            _layer_block(layer, k, tn, lambda j: ncb + j),
            pl.BlockSpec((nseq, conv - 1, tn), lambda i, j: (i // bps if nseq == 1 else 0, 0, j)),
            pl.BlockSpec((conv, tn), lambda i, j: (0, j)),
        ],
        out_specs=[
            pl.BlockSpec((tm, tn), lambda i, j: (i, j)),
            pl.BlockSpec((1, nseq, conv - 1, tn), lambda i, j: (i, 0, 0, j)),
        ],
        out_shape=[jax.ShapeDtypeStruct((r, dff), BF16),
                   jax.ShapeDtypeStruct((nrb, nseq, conv - 1, dff), F32)],
        scratch_shapes=[pltpu.VMEM((tm, k), BF16), pltpu.VMEM((tp + SUBLANES, tn), F32),
                        pltpu.VMEM((ncb, SUBLANES, tn), F32)],
        compiler_params=_cparams("arbitrary", "arbitrary"),
        name="ffn_up",
    )(x, g, w, w, st, wc)
    if nseq == 1:
        return act, tails.reshape(bsz, bps, conv - 1, dff)[:, bps - 1]
    return act, tails[0]


def _layer_pre(x, t, layer, depth, lw, kv_all, want_vt, cfg):
    outs = _in_proj(x, lw['g_mix'], lw['w_t'], lw['n_a'], 2 * cfg.d, lw['g_qk'], kv_all, layer, depth, cfg,
                    t_seq=t if want_vt else None)
    z, side, k_all, v_all = outs[:4]
    return z, side, (k_all, v_all), (outs[4] if len(outs) > 4 else None)


def _layer_post(x, z, ao, bsz, t, t_valid, layer, depth, lw, st_qk, st_ff, c0, n0, m0, s0, c_all, cfg):
    mw = cfg.mw
    g, gt = _gates(x, lw['g_mix'], lw['w_t'], lw['n_a'], layer, lw['b_if'], bsz, t, cfg)
    hm, c_all, n_new, m_new = _mlstm(z, g, gt, lw['w_qkconv'], st_qk, c0, n0, m0, s0, lw['g_mout'], c_all, layer,
                                     depth, bsz, t, t_valid, cfg)
    merged = _gated_dual(ao, hm, z, lw['w_br_attn'], lw['w_br_mlstm'], layer, 4 * mw)
    x1 = _mm_res(merged, lw['w_out'], x, layer, tn_pref=512, name="out_proj")
    act, ff_new = _ffn_up(x1, lw['g_ffn'], lw['w_ff_in'], layer, st_ff, lw['w_ffconv'], bsz, t, t_valid, cfg)
    x2 = _mm_res(act, lw['w_ff_out'], x1, layer, tn_pref=256, name="ffn_down")

    qk_new = z.reshape(bsz, t, -1)[:, t_valid - (cfg.conv - 1):t_valid, :2 * mw]
    return x2, c_all, (n_new, m_new, qk_new, ff_new)


def _layer_weights(l, cfg, w_t, g_mix, g_q, g_k, w_lambda, g_sub, w_qkconv, b_if, g_mout, w_br_attn,
                   w_br_mlstm, w_out, g_ffn, w_ff_in, w_ffconv, w_ff_out):
    qw, vw, mw, d = cfg.qw, cfg.vw, cfg.mw, cfg.d
    ng = 2 * cfg.mh
    reps = qw // cfg.e
    return {
        'g_mix': g_mix[l].reshape(1, d),
        'w_t': w_t,
        'n_a': 2 * qw + vw + 4 * mw,
        'b_if': b_if[l].reshape(ng),
        'g_qk': jnp.concatenate([jnp.tile(g_q[l], reps), jnp.tile(g_k[l], reps)]).reshape(1, 2 * qw),
        'w_lambda': w_lambda[l],
        'g_sub': g_sub[l].reshape(1, cfg.dv),
        'w_qkconv': w_qkconv[l],
        'g_mout': g_mout[l],
        'w_br_attn': w_br_attn,
        'w_br_mlstm': w_br_mlstm,
        'w_out': w_out,
        'g_ffn': g_ffn[l].reshape(1, d),
        'w_ff_in': w_ff_in,
        'w_ffconv': w_ffconv[l],
        'w_ff_out': w_ff_out,
    }


def kernel(x_prompt, x_sample, cache_k, cache_v, page_table, state_C, state_n, state_m, state_qkconv,
           state_ffconv, g_mix, w_in, g_q, g_k, w_lambda, g_sub, w_qkconv, b_if, g_mout, w_br_attn,
           w_br_mlstm, w_out, g_ffn, w_ff_in, w_ffconv, w_ff_out):
    bp, tp_, d = x_prompt.shape
    bs, ts, _ = x_sample.shape
    depth = w_in.shape[0]
    cfg = Cfg(d=d, ah=cache_k.shape[3], e=cache_k.shape[4] // 2, dv=cache_v.shape[4], mh=state_C.shape[2],
              dh=state_C.shape[3], conv=w_qkconv.shape[1], dff=w_ff_out.shape[1], ffconv=w_ffconv.shape[1])
    assert ts <= SAMPLE_TPAD and ts >= max(cfg.conv, cfg.ffconv) - 1

    yp = x_prompt.reshape(bp * tp_, d)
    ys = jnp.zeros((bs, SAMPLE_TPAD, d), F32).at[:, :ts].set(x_sample).reshape(bs * SAMPLE_TPAD, d)
    zero_qk = jnp.zeros((bp, cfg.conv - 1, 2 * cfg.mw), F32)
    zero_ff = jnp.zeros((bp, cfg.ffconv - 1, cfg.dff), F32)
    zero_c = jnp.zeros((bp * cfg.mh, cfg.dh, cfg.dh), F32)
    zero_n = jnp.zeros((bp * cfg.mh, 1, cfg.dh), F32)
    zero_m = jnp.zeros((bp, cfg.mh), F32)
    all_c = state_C.reshape(depth * bs * cfg.mh, cfg.dh, cfg.dh)
    all_n = state_n.reshape(depth * bs * cfg.mh, 1, cfg.dh)

    new_p = [[] for _ in range(4)]
    new_s = [[] for _ in range(4)]
    kv_p = kv_s = c_p = c_s = None
    w_t = jnp.swapaxes(w_in, 1, 2)
    for l in range(depth):
        lw = _layer_weights(l, cfg, w_t, g_mix, g_q, g_k, w_lambda, g_sub, w_qkconv, b_if, g_mout,
                            w_br_attn, w_br_mlstm, w_out, g_ffn, w_ff_in, w_ffconv, w_ff_out)
        z_p, side_p, kv_p, vt_p = _layer_pre(yp, tp_, l, depth, lw, kv_p, True, cfg)
        z_s, side_s, kv_s, _ = _layer_pre(ys, SAMPLE_TPAD, l, depth, lw, kv_s, False, cfg)
        ao_p, ao_s = _attention(side_p, vt_p, side_s, cache_k, cache_v, page_table, l, lw['w_lambda'], lw['g_sub'],
                                bp, tp_, ts, cfg, 0.8 - 0.6 * math.exp(-0.3 * l))
        yp, c_p, sp = _layer_post(yp, z_p, ao_p, bp, tp_, tp_, l, depth, lw, zero_qk, zero_ff, zero_c, zero_n,
                                  zero_m, 0, c_p, cfg)
        ys, c_s, ss = _layer_post(ys, z_s, ao_s, bs, SAMPLE_TPAD, ts, l, depth, lw, state_qkconv[l],
                                  state_ffconv[l], all_c, all_n, state_m[l], l * bs * cfg.mh, c_s, cfg)
        for j in range(4):
            new_p[j].append(sp[j])
            new_s[j].append(ss[j])
    outs_p = [c_p.reshape(depth, bp, cfg.mh, cfg.dh, cfg.dh)] + [jnp.stack(a, axis=0) for a in new_p]
    outs_s = [c_s.reshape(depth, bs, cfg.mh, cfg.dh, cfg.dh)] + [jnp.stack(a, axis=0) for a in new_s]
    kp = kv_p[0].reshape(depth, bp, tp_, cfg.ah, 2 * cfg.e)
    vp = kv_p[1].reshape(depth, bp, tp_, cfg.ah, cfg.dv)
    ks = kv_s[0].reshape(depth, bs, SAMPLE_TPAD, cfg.ah, 2 * cfg.e)[:, :, :ts]
    vs = kv_s[1].reshape(depth, bs, SAMPLE_TPAD, cfg.ah, cfg.dv)[:, :, :ts]
    y_prompt = yp.reshape(bp, tp_, d)
    y_sample = ys.reshape(bs, SAMPLE_TPAD, d)[:, :ts]
    return (y_prompt, y_sample, kp, vp, *outs_p, ks, vs, *outs_s)
```
